```python
import math
import jax, jax.numpy as jnp
from jax import lax
import numpy as np

D_MODEL = 1024
BATCH = 8
SEQ = 2048
DEPTH = 4
DEC_BATCH = 32
DEC_SEQ = 1
PAST_LEN = 16384
PAGE_SIZE = 128

N_META = 16
N_A = DEPTH // 2
N_B = DEPTH - N_A
CONV_W = 31
N_HEADS = 16
QK_NOPE = 64
QK_ROPE = 32
V_HEAD = 64
KV_LORA = 256
Q_LORA = 384
D_FF = -(-8 * D_MODEL // (3 * 256)) * 256
ROPE_BASE = 10000.0
EPS = 1e-6
Q_BLOCK = 128
SCALE = 1.0 / math.sqrt(QK_NOPE + QK_ROPE)

kernel_name = "yoco_conformer_conv_mla_decoder_step"


def rmsnorm(x, g):
    xf = x.astype(jnp.float32)
    y = xf * lax.rsqrt(jnp.mean(xf * xf, axis=-1, keepdims=True) + EPS)
    return (y * g.astype(jnp.float32)).astype(x.dtype)


def layernorm(x, g, b):
    xf = x.astype(jnp.float32)
    mu = jnp.mean(xf, axis=-1, keepdims=True)
    xc = xf - mu
    y = xc * lax.rsqrt(jnp.mean(xc * xc, axis=-1, keepdims=True) + EPS)
    return (y * g.astype(jnp.float32) + b.astype(jnp.float32)).astype(x.dtype)


def rope(x, pos):
    half = QK_ROPE // 2
    inv = jnp.power(ROPE_BASE, -jnp.arange(half, dtype=jnp.float32) / half)
    ang = pos.astype(jnp.float32)[:, None] * inv[None, :]
    cos = jnp.cos(ang)[:, None, :]
    sin = jnp.sin(ang)[:, None, :]
    xf = x.astype(jnp.float32)
    x1, x2 = xf[..., :half], xf[..., half:]
    return jnp.concatenate([x1 * cos - x2 * sin, x1 * sin + x2 * cos], axis=-1).astype(x.dtype)


def swiglu(h, w_gate, w_up, w_down):
    return (jax.nn.silu(h @ w_gate) * (h @ w_up)) @ w_down


def conv_module(h, state, pw1_w, pw1_b, dw_w, dw_b, ln_g, ln_b, pw2_w, pw2_b):
    u = h @ pw1_w + pw1_b
    g = u[..., :D_MODEL] * jax.nn.sigmoid(u[..., D_MODEL:])
    buf = jnp.concatenate([state.astype(g.dtype), g], axis=1)
    y = lax.conv_general_dilated(
        buf, dw_w[:, None, :], window_strides=(1,), padding='VALID',
        dimension_numbers=('NWC', 'WIO', 'NWC'), feature_group_count=D_MODEL) + dw_b
    y = jax.nn.silu(layernorm(y, ln_g, ln_b))
    return y @ pw2_w + pw2_b, buf[:, -(CONV_W - 1):]


def kv_side(h, pos, kv_norm_g, w_dkv, kv_latent_norm_g):
    ckr = rmsnorm(h, kv_norm_g) @ w_dkv
    c = rmsnorm(ckr[..., :KV_LORA], kv_latent_norm_g)
    kr = rope(ckr[..., KV_LORA:][:, :, None, :], pos)[:, :, 0]
    return c, kr


def mla_queries(hn, pos, w_dq, q_norm_g, w_uq):
    cq = rmsnorm(hn @ w_dq, q_norm_g)
    q = jnp.einsum('btr,rhd->bthd', cq, w_uq)
    return q[..., :QK_NOPE], rope(q[..., QK_NOPE:], pos)


def attend_prompt(q_nope, q_rope, k_nope, k_rope, v):
    b, t = q_nope.shape[0], q_nope.shape[1]
    n_blk = -(-t // Q_BLOCK)
    tp = n_blk * Q_BLOCK
    pad = ((0, 0), (0, tp - t), (0, 0), (0, 0))
    qn = jnp.pad(q_nope, pad).reshape(b, n_blk, Q_BLOCK, N_HEADS, QK_NOPE).transpose(1, 0, 2, 3, 4)
    qr = jnp.pad(q_rope, pad).reshape(b, n_blk, Q_BLOCK, N_HEADS, QK_ROPE).transpose(1, 0, 2, 3, 4)
    kpos = jnp.arange(t)

    def block(args):
        i, qn_b, qr_b = args
        s = (jnp.einsum('bqhd,bkhd->bhqk', qn_b, k_nope)
             + jnp.einsum('bqhr,bkr->bhqk', qr_b, k_rope)).astype(jnp.float32) * SCALE
        qpos = i * Q_BLOCK + jnp.arange(Q_BLOCK)
        s = jnp.where(kpos[None, :] <= qpos[:, None], s, -jnp.inf)
        p = jax.nn.softmax(s, axis=-1).astype(v.dtype)
        return jnp.einsum('bhqk,bkhd->bqhd', p, v)

    o = lax.map(block, (jnp.arange(n_blk), qn, qr))
    o = o.transpose(1, 0, 2, 3, 4).reshape(b, tp, N_HEADS * V_HEAD)
    return o[:, :t]


def attend_sample(q_nope, q_rope, c_past, kr_past, c_new, kr_new, w_uk, w_uv):
    b, n = q_nope.shape[0], q_nope.shape[1]
    n_past = c_past.shape[1]
    q_lat = jnp.einsum('bqhd,chd->bqhc', q_nope, w_uk)
    s_past = (jnp.einsum('bqhc,bkc->bhqk', q_lat, c_past)
              + jnp.einsum('bqhr,bkr->bhqk', q_rope, kr_past))
    s_new = (jnp.einsum('bqhc,bkc->bhqk', q_lat, c_new)
             + jnp.einsum('bqhr,bkr->bhqk', q_rope, kr_new))
    causal = jnp.tril(jnp.ones((n, n), dtype=bool))
    s_new = jnp.where(causal, s_new.astype(jnp.float32), -jnp.inf)
    s = jnp.concatenate([s_past.astype(jnp.float32), s_new], axis=-1) * SCALE
    p = jax.nn.softmax(s, axis=-1).astype(c_past.dtype)
    o_lat = (jnp.einsum('bhqk,bkc->bqhc', p[..., :n_past], c_past)
             + jnp.einsum('bhqk,bkc->bqhc', p[..., n_past:], c_new))
    o = jnp.einsum('bqhc,chd->bqhd', o_lat, w_uv)
    return o.reshape(b, n, N_HEADS * V_HEAD)


def setup_inputs(seed: int = 0) -> dict:
    key = jax.random.key(seed)
    k = jax.random.split(key, 31)
    f32 = jnp.float32
    n_pages = PAST_LEN // PAGE_SIZE
    n_used = DEC_BATCH * n_pages
    n_pool = n_used + n_used // 4

    def nrm(i, shape, scale=1.0):
        return scale * jax.random.normal(k[i], shape, f32)

    def gain(i, shape):
        return 1.0 + nrm(i, shape, 0.05)

    def bias(i, shape):
        return nrm(i, shape, 0.02)

    page_table = jax.random.permutation(k[5], n_pool)[:n_used].reshape(DEC_BATCH, n_pages).astype(jnp.int32)
    return {
        "x_prompt": nrm(0, (BATCH, SEQ, D_MODEL)),
        "x_sample": nrm(1, (DEC_BATCH, DEC_SEQ, D_MODEL)),
        "cache_latent": nrm(2, (n_pool, PAGE_SIZE, KV_LORA)),
        "cache_krope": nrm(3, (n_pool, PAGE_SIZE, QK_ROPE)),
        "state_conv": nrm(4, (N_A, DEC_BATCH, CONV_W - 1, D_MODEL), 0.5),
        "page_table": page_table,
        "meta_tokens": nrm(6, (N_META, D_MODEL)),
        "a_norm_g": gain(7, (N_A, D_MODEL)),
        "a_pw1_w": nrm(8, (N_A, D_MODEL, 2 * D_MODEL), D_MODEL ** -0.5),
        "a_pw1_b": bias(9, (N_A, 2 * D_MODEL)),
        "a_dw_w": nrm(10, (N_A, CONV_W, D_MODEL), CONV_W ** -0.5),
        "a_dw_b": bias(11, (N_A, D_MODEL)),
        "a_ln_g": gain(12, (N_A, D_MODEL)),
        "a_ln_b": bias(13, (N_A, D_MODEL)),
        "a_pw2_w": nrm(14, (N_A, D_MODEL, D_MODEL), D_MODEL ** -0.5),
        "a_pw2_b": bias(15, (N_A, D_MODEL)),
        "ffn_norm_g": gain(16, (DEPTH, D_MODEL)),
        "ffn_w_gate": nrm(17, (DEPTH, D_MODEL, D_FF), D_MODEL ** -0.5),
        "ffn_w_up": nrm(18, (DEPTH, D_MODEL, D_FF), D_MODEL ** -0.5),
        "ffn_w_down": nrm(19, (DEPTH, D_FF, D_MODEL), D_FF ** -0.5),
        "kv_norm_g": gain(20, (D_MODEL,)),
        "w_dkv": nrm(21, (D_MODEL, KV_LORA + QK_ROPE), D_MODEL ** -0.5),
        "kv_latent_norm_g": gain(22, (KV_LORA,)),
        "w_uk": nrm(23, (KV_LORA, N_HEADS, QK_NOPE), KV_LORA ** -0.5),
        "w_uv": nrm(24, (KV_LORA, N_HEADS, V_HEAD), KV_LORA ** -0.5),
        "b_norm_g": gain(25, (N_B, D_MODEL)),
        "b_w_dq": nrm(26, (N_B, D_MODEL, Q_LORA), D_MODEL ** -0.5),
        "b_q_norm_g": gain(27, (N_B, Q_LORA)),
        "b_w_uq": nrm(28, (N_B, Q_LORA, N_HEADS, QK_NOPE + QK_ROPE), Q_LORA ** -0.5),
        "b_w_o": nrm(29, (N_B, N_HEADS * V_HEAD, D_MODEL), (N_HEADS * V_HEAD) ** -0.5),
        "final_norm_g": gain(30, (D_MODEL,)),
    }


def reference(x_prompt, x_sample, cache_latent, cache_krope, state_conv, page_table,
              meta_tokens, a_norm_g, a_pw1_w, a_pw1_b, a_dw_w, a_dw_b, a_ln_g, a_ln_b,
              a_pw2_w, a_pw2_b, ffn_norm_g, ffn_w_gate, ffn_w_up, ffn_w_down,
              kv_norm_g, w_dkv, kv_latent_norm_g, w_uk, w_uv,
              b_norm_g, b_w_dq, b_q_norm_g, b_w_uq, b_w_o, final_norm_g):

    def trunk(x, pos, conv_state, make_attend):
        new_conv = []
        c = kr = None
        attn = None
        for l in range(DEPTH):
            if l < N_A:
                y, st = conv_module(rmsnorm(x, a_norm_g[l]), conv_state[l], a_pw1_w[l], a_pw1_b[l],
                                    a_dw_w[l], a_dw_b[l], a_ln_g[l], a_ln_b[l], a_pw2_w[l], a_pw2_b[l])
                new_conv.append(st)
            else:
                if l == N_A:
                    c, kr = kv_side(x, pos, kv_norm_g, w_dkv, kv_latent_norm_g)
                    attn = make_attend(c, kr)
                j = l - N_A
                qn, qr = mla_queries(rmsnorm(x, b_norm_g[j]), pos, b_w_dq[j], b_q_norm_g[j], b_w_uq[j])
                y = attn(qn, qr) @ b_w_o[j]
            x = x + y
            x = x + swiglu(rmsnorm(x, ffn_norm_g[l]), ffn_w_gate[l], ffn_w_up[l], ffn_w_down[l])
        return rmsnorm(x, final_norm_g), jnp.stack(new_conv), c, kr

    bp = x_prompt.shape[0]
    xp = jnp.concatenate([jnp.broadcast_to(meta_tokens[None].astype(x_prompt.dtype), (bp, N_META, D_MODEL)), x_prompt], axis=1)
    pos_p = jnp.arange(xp.shape[1])
    conv0_p = jnp.zeros((N_A, bp, CONV_W - 1, D_MODEL), xp.dtype)

    def prompt_attend(c, kr):
        k_nope = jnp.einsum('btc,chd->bthd', c, w_uk)
        v = jnp.einsum('btc,chd->bthd', c, w_uv)
        return lambda qn, qr: attend_prompt(qn, qr, k_nope, kr, v)

    out_p, conv_p, c_p, kr_p = trunk(xp, pos_p, conv0_p, prompt_attend)
    y_prompt = out_p[:, N_META:]

    bs, n_new = x_sample.shape[0], x_sample.shape[1]
    pos_s = PAST_LEN + jnp.arange(n_new)
    c_past = cache_latent[page_table].reshape(bs, -1, KV_LORA)
    kr_past = cache_krope[page_table].reshape(bs, -1, QK_ROPE)

    def sample_attend(c, kr):
        return lambda qn, qr: attend_sample(qn, qr, c_past, kr_past, c, kr, w_uk, w_uv)

    y_sample, conv_s, c_s, kr_s = trunk(x_sample, pos_s, state_conv, sample_attend)

    return (y_prompt, y_sample, c_p, kr_p, conv_p, c_s, kr_s, conv_s)
```

```python
import functools
import math

import jax
import jax.numpy as jnp
from jax import lax
from jax.experimental import pallas as pl
from jax.experimental.pallas import tpu as pltpu

D_MODEL = 1024
N_HEADS = 16
QK_NOPE = 64
QK_ROPE = 32
V_HEAD = 64
KV_LORA = 256
Q_LORA = 384
CONV_W = 31
N_META = 16
PAGE_SIZE = 128
ROPE_BASE = 10000.0
EPS = 1e-6
SCALE = 1.0 / math.sqrt(QK_NOPE + QK_ROPE)

HEAD_PAD = 128
HP = N_HEADS * HEAD_PAD

BF16 = jnp.bfloat16
F32 = jnp.float32

VMEM_LIMIT = 56 * 1024 * 1024

ROW_TILE = 384
SEQ_TILE = 688
KV_CHUNK = 256
FFN_CHUNK = 256
CONV_ROWS = 64
CONV_COLS = 256
HIST = 32
PAGES_PER_STEP = 8


def _params(*sem):
    return pltpu.CompilerParams(dimension_semantics=sem, vmem_limit_bytes=VMEM_LIMIT)


def _const_spec(shape):
    n = len(shape)
    return pl.BlockSpec(shape, lambda *_: (0,) * n, pipeline_mode=pl.Buffered(1))


def _rms(x, g):
    return x * lax.rsqrt(jnp.mean(x * x, axis=-1, keepdims=True) + EPS) * g


def _sigmoid(x):
    return 1.0 / (1.0 + jnp.exp(-x))


def _dot(a, b):
    return jnp.dot(a, b, preferred_element_type=F32)


def _dot_t(a, b):
    return lax.dot_general(a, b, (((1,), (1,)), ((), ())), preferred_element_type=F32)


def _pw1_kernel(x_ref, g_ref, w_ref, b_ref, o_ref):
    h = _rms(x_ref[...], g_ref[...]).astype(BF16)
    u = _dot(h, w_ref[...]) + b_ref[...]
    o_ref[...] = u[:, :D_MODEL] * _sigmoid(u[:, D_MODEL:])


def _pw1_glu(x, g, w, b, tm):
    m = x.shape[0]
    return pl.pallas_call(
        _pw1_kernel,
        grid=(m // tm,),
        in_specs=[
            pl.BlockSpec((tm, D_MODEL), lambda i: (i, 0)),
            _const_spec((1, D_MODEL)),
            _const_spec((D_MODEL, 2 * D_MODEL)),
            _const_spec((1, 2 * D_MODEL)),
        ],
        out_specs=pl.BlockSpec((tm, D_MODEL), lambda i: (i, 0)),
        out_shape=jax.ShapeDtypeStruct((m, D_MODEL), F32),
        compiler_params=_params("parallel"),
        name="pw1_glu",
    )(x, g, w, b)


def _ln_swish_pw2(y, x, lng_ref, lnb_ref, w2_ref, b2_ref):
    mu = jnp.mean(y, axis=-1, keepdims=True)
    yc = y - mu
    yn = yc * lax.rsqrt(jnp.mean(yc * yc, axis=-1, keepdims=True) + EPS)
    yn = yn * lng_ref[...] + lnb_ref[...]
    z = (yn * _sigmoid(yn)).astype(BF16)
    return x + _dot(z, w2_ref[...]) + b2_ref[...]


def _conv_kernel(g_ref, x_ref, dw_ref, dwb_ref, lng_ref, lnb_ref, w2_ref, b2_ref, o_ref,
                 buf, y_scr, *, tt, n_blocks):
    t = pl.program_id(1)

    @pl.when(t == 0)
    def _():
        buf[...] = jnp.zeros_like(buf)

    @pl.when(t > 0)
    def _():
        buf[0:HIST, :] = buf[tt:tt + HIST, :]

    buf[HIST:HIST + tt, :] = g_ref[...]

    def row_block(i, carry):
        rs = pl.multiple_of(i * CONV_ROWS, CONV_ROWS)
        for cb in range(D_MODEL // CONV_COLS):
            cols = slice(cb * CONV_COLS, (cb + 1) * CONV_COLS)
            win = buf[pl.ds(rs, CONV_ROWS + 40), cols]
            acc = jnp.zeros((CONV_ROWS, CONV_COLS), F32)
            for r in range(8):
                xr = win[r:r + CONV_ROWS + 32, :]
                for a in range(5):
                    o = 8 * a + r
                    if o < 2 or o > 32:
                        continue
                    acc = acc + xr[8 * a:8 * a + CONV_ROWS, :] * dw_ref[o:o + 1, cols]
            y_scr[pl.ds(rs, CONV_ROWS), cols] = acc + dwb_ref[:, cols]
        return carry

    lax.fori_loop(0, n_blocks, row_block, 0)
    o_ref[...] = _ln_swish_pw2(y_scr[0:tt, :], x_ref[...], lng_ref, lnb_ref, w2_ref, b2_ref)


def _conv_block(glu, x, dw_pad, dwb, lng, lnb, w2, b2, tt):
    b, t, _ = x.shape
    n_blocks = pl.cdiv(tt, CONV_ROWS)
    rows = n_blocks * CONV_ROWS
    tile = pl.BlockSpec((None, tt, D_MODEL), lambda bi, ti: (bi, ti, 0))
    return pl.pallas_call(
        functools.partial(_conv_kernel, tt=tt, n_blocks=n_blocks),
        grid=(b, t // tt),
        in_specs=[
            tile, tile,
            _const_spec((40, D_MODEL)),
            _const_spec((1, D_MODEL)),
            _const_spec((1, D_MODEL)),
            _const_spec((1, D_MODEL)),
            _const_spec((D_MODEL, D_MODEL)),
            _const_spec((1, D_MODEL)),
        ],
        out_specs=tile,
        out_shape=jax.ShapeDtypeStruct((b, t, D_MODEL), F32),
        scratch_shapes=[
            pltpu.VMEM((rows + 40, D_MODEL), F32),
            pltpu.VMEM((rows, D_MODEL), F32),
        ],
        compiler_params=_params("parallel", "arbitrary"),
        name="conv_block",
    )(glu, x, dw_pad, dwb, lng, lnb, w2, b2)


def _conv_step_kernel(st_ref, g_ref, x_ref, dw_ref, dwb_ref, lng_ref, lnb_ref, w2_ref, b2_ref, o_ref):
    g = g_ref[...]
    y = g * dw_ref[CONV_W - 1:CONV_W, :] + dwb_ref[...]
    for k in range(CONV_W - 1):
        y = y + st_ref[:, k, :] * dw_ref[k:k + 1, :]
    o_ref[...] = _ln_swish_pw2(y, x_ref[...], lng_ref, lnb_ref, w2_ref, b2_ref)


def _conv_step(state, glu, x, dw, dwb, lng, lnb, w2, b2):
    m = x.shape[0]
    return pl.pallas_call(
        _conv_step_kernel,
        out_shape=jax.ShapeDtypeStruct((m, D_MODEL), F32),
        compiler_params=pltpu.CompilerParams(vmem_limit_bytes=VMEM_LIMIT),
        name="conv_step",
    )(state, glu, x, dw, dwb, lng, lnb, w2, b2)


def _ffn_kernel(*refs, pre, final, d_ff):
    it = iter(refs)
    x_ref = next(it)
    if pre:
        o_ref, wo_ref = next(it), next(it)
    g_ref, wg_ref, wu_ref, wd_ref = next(it), next(it), next(it), next(it)
    if final:
        gf_ref = next(it)
    out_ref, a_scr = next(it), next(it)

    x = x_ref[...]
    if pre:
        x = x + _dot(o_ref[...], wo_ref[...])
    h = _rms(x, g_ref[...]).astype(BF16)
    for c in range(d_ff // FFN_CHUNK):
        sl = slice(c * FFN_CHUNK, (c + 1) * FFN_CHUNK)
        gate = _dot(h, wg_ref[:, sl])
        up = _dot(h, wu_ref[:, sl])
        a_scr[:, sl] = (gate * _sigmoid(gate) * up).astype(BF16)
    y = x + _dot(a_scr[...], wd_ref[...])
    if final:
        y = _rms(y, gf_ref[...])
    out_ref[...] = y


def _ffn(x, g, wg, wu, wd, tm, attn=None, final_g=None):
    m = x.shape[0]
    d_ff = wg.shape[1]
    row = lambda i: (i, 0)
    args = [x]
    specs = [pl.BlockSpec((tm, D_MODEL), row)]
    if attn is not None:
        o, wo = attn
        args += [o, wo]
        specs += [pl.BlockSpec((tm, HP), row), _const_spec((HP, D_MODEL))]
    args += [g, wg, wu, wd]
    specs += [_const_spec((1, D_MODEL)), _const_spec((D_MODEL, d_ff)),
              _const_spec((D_MODEL, d_ff)), _const_spec((d_ff, D_MODEL))]
    if final_g is not None:
        args.append(final_g)
        specs.append(_const_spec((1, D_MODEL)))
    return pl.pallas_call(
        functools.partial(_ffn_kernel, pre=attn is not None, final=final_g is not None, d_ff=d_ff),
        grid=(m // tm,),
        in_specs=specs,
        out_specs=pl.BlockSpec((tm, D_MODEL), row),
        out_shape=jax.ShapeDtypeStruct((m, D_MODEL), F32),
        scratch_shapes=[pltpu.VMEM((tm, d_ff), BF16)],
        compiler_params=_params("parallel"),
        name="ffn",
    )(*args)


def _kv_kernel(*refs, heads, n_real):
    if heads:
        (x_ref, g_ref, wc_ref, wr_ref, gl_ref, cos_ref, sin_ref, wk_ref, wv_ref, e_ref,
         c_ref, kr_ref, k_ref, v_ref) = refs
    else:
        x_ref, g_ref, wc_ref, wr_ref, gl_ref, cos_ref, sin_ref, c_ref, kr_ref = refs

    def real():
        h = _rms(x_ref[...], g_ref[...]).astype(BF16)
        c = _rms(_dot(h, wc_ref[...]), gl_ref[...])
        r = _dot(h, wr_ref[...])
        kr = r[:, :QK_ROPE] * cos_ref[...] + r[:, QK_ROPE:] * sin_ref[...]
        c_ref[...] = c
        kr_ref[...] = kr
        if heads:
            cb = c.astype(BF16)
            k_ref[...] = (_dot(cb, wk_ref[...]) + _dot(kr.astype(BF16), e_ref[...])).astype(BF16)
            v_ref[...] = _dot(cb, wv_ref[...]).astype(BF16)

    if not heads:
        real()
        return

    t = pl.program_id(1)
    pl.when(t < n_real)(real)

    @pl.when(t >= n_real)
    def _():
        k_ref[...] = jnp.zeros_like(k_ref)
        v_ref[...] = jnp.zeros_like(v_ref)


def _kv_side(x, g, wc, wr, gl, cos, sin, tt, heads=None, t_pad=None):
    b, t, _ = x.shape
    n_real = t // tt
    n_steps = n_real if heads is None else pl.cdiv(t_pad, tt)
    clamp = lambda bi, ti: (bi, jnp.minimum(ti, n_real - 1), 0)
    tab = lambda bi, ti: (jnp.minimum(ti, n_real - 1), 0)
    args = [x, g, wc, wr, gl, cos, sin]
    specs = [
        pl.BlockSpec((None, tt, D_MODEL), clamp),
        _const_spec((1, D_MODEL)),
        _const_spec((D_MODEL, KV_LORA)),
        _const_spec((D_MODEL, 2 * QK_ROPE)),
        _const_spec((1, KV_LORA)),
        pl.BlockSpec((tt, QK_ROPE), tab),
        pl.BlockSpec((tt, QK_ROPE), tab),
    ]
    out_specs = [pl.BlockSpec((None, tt, KV_LORA), clamp), pl.BlockSpec((None, tt, QK_ROPE), clamp)]
    out_shape = [jax.ShapeDtypeStruct((b, t, KV_LORA), F32), jax.ShapeDtypeStruct((b, t, QK_ROPE), F32)]
    if heads is not None:
        args += list(heads)
        specs += [_const_spec((KV_LORA, HP)), _const_spec((KV_LORA, HP)), _const_spec((QK_ROPE, HP))]
        full = lambda bi, ti: (bi, ti, 0)
        out_specs += [pl.BlockSpec((None, tt, HP), full)] * 2
        out_shape += [jax.ShapeDtypeStruct((b, t_pad, HP), BF16)] * 2
    return pl.pallas_call(
        functools.partial(_kv_kernel, heads=heads is not None, n_real=n_real),
        grid=(b, n_steps),
        in_specs=specs,
        out_specs=out_specs,
        out_shape=out_shape,
        compiler_params=_params("parallel", "arbitrary"),
        name="kv_side",
    )(*args)


def _q_kernel(x_ref, g_ref, wdq_ref, gq_ref, wq_ref, wqr_ref, cos_ref, sin_ref, q_ref):
    h = _rms(x_ref[...], g_ref[...]).astype(BF16)
    cq = _rms(_dot(h, wdq_ref[...]), gq_ref[...]).astype(BF16)
    q = _dot(cq, wq_ref[...])
    qr = _dot(cq, wqr_ref[...])
    cos = cos_ref[...]
    sin = sin_ref[...]
    for hd in range(N_HEADS):
        sl = slice(hd * HEAD_PAD, (hd + 1) * HEAD_PAD)
        q_ref[:, sl] = ((q[:, sl] * cos + qr[:, sl] * sin) * SCALE).astype(BF16)


def _q_proj(x, g, wdq, gq, wq, wqr, cos, sin, tt):
    b, t, _ = x.shape
    full = lambda bi, ti: (bi, ti, 0)
    tab = lambda bi, ti: (ti, 0)
    return pl.pallas_call(
        _q_kernel,
        grid=(b, t // tt),
        in_specs=[
            pl.BlockSpec((None, tt, D_MODEL), full),
            _const_spec((1, D_MODEL)),
            _const_spec((D_MODEL, Q_LORA)),
            _const_spec((1, Q_LORA)),
            _const_spec((Q_LORA, HP)),
            _const_spec((Q_LORA, HP)),
            pl.BlockSpec((tt, HEAD_PAD), tab),
            pl.BlockSpec((tt, HEAD_PAD), tab),
        ],
        out_specs=pl.BlockSpec((None, tt, HP), full),
        out_shape=jax.ShapeDtypeStruct((b, t, HP), BF16),
        compiler_params=_params("parallel", "parallel"),
        name="q_proj",
    )(x, g, wdq, gq, wq, wqr, cos, sin)


def _attn_kernel(q_ref, k_ref, v_ref, o_ref, m_scr, l_scr, acc_scr, *, tq):
    i = pl.program_id(2)
    q = q_ref[...]
    m_scr[...] = jnp.full_like(m_scr, -jnp.inf)
    l_scr[...] = jnp.zeros_like(l_scr)
    acc_scr[...] = jnp.zeros_like(acc_scr)
    q_lo = i * tq
    n_unmasked = q_lo // KV_CHUNK
    n_all = (q_lo + tq + KV_CHUNK - 1) // KV_CHUNK

    def step(c, masked):
        ks = pl.multiple_of(c * KV_CHUNK, KV_CHUNK)
        k = k_ref[pl.ds(ks, KV_CHUNK), :]
        v = v_ref[pl.ds(ks, KV_CHUNK), :]
        s = _dot_t(q, k)
        if masked:
            qpos = q_lo + lax.broadcasted_iota(jnp.int32, s.shape, 0)
            kpos = ks + lax.broadcasted_iota(jnp.int32, s.shape, 1)
            s = jnp.where(kpos <= qpos, s, -jnp.inf)
        m_prev = m_scr[...]
        m_new = jnp.maximum(m_prev, jnp.max(s, axis=1, keepdims=True))
        alpha = jnp.exp(m_prev - m_new)
        p = jnp.exp(s - jnp.concatenate([m_new] * (KV_CHUNK // HEAD_PAD), axis=1))
        l_scr[...] = alpha * l_scr[...] + jnp.sum(p, axis=1, keepdims=True)
        acc_scr[...] = alpha * acc_scr[...] + _dot(p.astype(BF16), v)
        m_scr[...] = m_new

    def unmasked(c, carry):
        step(c, False)
        return carry

    def masked(c, carry):
        step(c, True)
        return carry

    lax.fori_loop(0, n_unmasked, unmasked, 0)
    lax.fori_loop(n_unmasked, n_all, masked, 0)
    o_ref[...] = (acc_scr[...] / l_scr[...]).astype(BF16)


def _attn_prompt(q, k, v, tq):
    b, t, _ = q.shape
    t_pad = k.shape[1]
    qo = pl.BlockSpec((None, tq, HEAD_PAD), lambda bi, hi, qi: (bi, qi, hi))
    kv = pl.BlockSpec((None, t_pad, HEAD_PAD), lambda bi, hi, qi: (bi, 0, hi))
    return pl.pallas_call(
        functools.partial(_attn_kernel, tq=tq),
        grid=(b, N_HEADS, t // tq),
        in_specs=[qo, kv, kv],
        out_specs=qo,
        out_shape=jax.ShapeDtypeStruct((b, t, HP), BF16),
        scratch_shapes=[pltpu.VMEM((tq, HEAD_PAD), F32)] * 3,
        compiler_params=_params("parallel", "parallel", "arbitrary"),
        name="attn_prompt",
    )(q, k, v)


def _qlat_kernel(q_ref, w_ref, o_ref):
    o_ref[...] = _dot(q_ref[...], w_ref[...]).astype(BF16)


def _q_latent(q, w_uk_t):
    m = q.shape[0]
    return pl.pallas_call(
        _qlat_kernel,
        grid=(N_HEADS,),
        in_specs=[pl.BlockSpec((m, HEAD_PAD), lambda h: (0, h)),
                  pl.BlockSpec((None, HEAD_PAD, KV_LORA), lambda h: (h, 0, 0))],
        out_specs=pl.BlockSpec((None, m, KV_LORA), lambda h: (h, 0, 0)),
        out_shape=jax.ShapeDtypeStruct((N_HEADS, m, KV_LORA), BF16),
        compiler_params=_params("parallel"),
        name="q_latent",
    )(q, w_uk_t)


def _decode_kernel(pt_ref, ql_ref, qr_ref, cn_ref, krn_ref, *refs):
    g = PAGES_PER_STEP
    lat_refs, kr_refs = refs[:g], refs[g:2 * g]
    o_ref, m_scr, l_scr, acc_scr = refs[2 * g:]
    step = pl.program_id(1)
    ql = ql_ref[...]
    qr = qr_ref[...]

    @pl.when(step == 0)
    def _():
        cn = cn_ref[...].astype(BF16).astype(F32)
        krn = krn_ref[...].astype(BF16).astype(F32)
        s_new = (jnp.sum(ql.astype(F32) * cn, axis=1, keepdims=True)
                 + jnp.sum(qr.astype(F32) * krn, axis=1, keepdims=True))
        m_scr[...] = jnp.broadcast_to(s_new, m_scr.shape)
        l_scr[...] = jnp.ones_like(l_scr)
        acc_scr[...] = jnp.broadcast_to(cn, acc_scr.shape)

    c = jnp.concatenate([r[...].astype(BF16) for r in lat_refs], axis=0)
    kr = jnp.concatenate([r[...].astype(BF16) for r in kr_refs], axis=0)
    s = _dot_t(ql, c) + _dot_t(qr, kr)
    m_prev = m_scr[...]
    m_new = jnp.maximum(m_prev, jnp.max(s, axis=1, keepdims=True))
    alpha = jnp.exp(m_prev - m_new)
    p = jnp.exp(s - jnp.concatenate([m_new] * (s.shape[1] // HEAD_PAD), axis=1))
    l_scr[...] = alpha * l_scr[...] + jnp.sum(p, axis=1, keepdims=True)
    acc_scr[...] = jnp.concatenate([alpha, alpha], axis=1) * acc_scr[...] + _dot(p.astype(BF16), c)
    m_scr[...] = m_new

    @pl.when(step == pl.num_programs(1) - 1)
    def _():
        o_ref[...] = acc_scr[...] / jnp.concatenate([l_scr[...]] * 2, axis=1)


def _decode_attn(page_table, q_lat, q_rope, c_new, kr_new, cache_latent, cache_krope):
    b, n_pages = page_table.shape
    g = PAGES_PER_STEP
    per_seq = lambda width: pl.BlockSpec((None, N_HEADS, width), lambda bi, si, pt: (bi, 0, 0))
    new_tok = lambda width: pl.BlockSpec((None, 1, width), lambda bi, si, pt: (bi, 0, 0))

    def page(width, j):
        return pl.BlockSpec((None, PAGE_SIZE, width), lambda bi, si, pt: (pt[bi, si * g + j], 0, 0))

    grid_spec = pltpu.PrefetchScalarGridSpec(
        num_scalar_prefetch=1,
        grid=(b, n_pages // g),
        in_specs=([per_seq(KV_LORA), per_seq(QK_ROPE), new_tok(KV_LORA), new_tok(QK_ROPE)]
                  + [page(KV_LORA, j) for j in range(g)] + [page(QK_ROPE, j) for j in range(g)]),
        out_specs=per_seq(KV_LORA),
        scratch_shapes=[pltpu.VMEM((N_HEADS, HEAD_PAD), F32), pltpu.VMEM((N_HEADS, HEAD_PAD), F32),
                        pltpu.VMEM((N_HEADS, KV_LORA), F32)],
    )
    return pl.pallas_call(
        _decode_kernel,
        grid_spec=grid_spec,
        out_shape=jax.ShapeDtypeStruct((b, N_HEADS, KV_LORA), F32),
        compiler_params=_params("parallel", "arbitrary"),
        name="decode_attn",
    )(page_table, q_lat, q_rope, c_new, kr_new, *([cache_latent] * g), *([cache_krope] * g))


def _olat_kernel(o_ref, w_ref, out_ref):
    out_ref[...] = _dot(o_ref[...].astype(BF16), w_ref[...]).astype(BF16)


def _o_latent_proj(o_lat, w_uv_pad):
    m = o_lat.shape[1]
    return pl.pallas_call(
        _olat_kernel,
        grid=(N_HEADS,),
        in_specs=[pl.BlockSpec((None, m, KV_LORA), lambda h: (h, 0, 0)),
                  pl.BlockSpec((None, KV_LORA, HEAD_PAD), lambda h: (h, 0, 0))],
        out_specs=pl.BlockSpec((m, HEAD_PAD), lambda h: (0, h)),
        out_shape=jax.ShapeDtypeStruct((m, HP), BF16),
        compiler_params=_params("parallel"),
        name="o_latent_proj",
    )(o_lat, w_uv_pad)


def _rope_tables(pos):
    half = QK_ROPE // 2
    inv = jnp.power(ROPE_BASE, -jnp.arange(half, dtype=F32) / half)
    ang = pos.astype(F32)[:, None] * inv[None, :]
    cos = jnp.cos(ang)
    sin = jnp.sin(ang)
    cos32 = jnp.concatenate([cos, cos], axis=1)
    sin32 = jnp.concatenate([sin, sin], axis=1)
    n = pos.shape[0]
    cos128 = jnp.concatenate([jnp.ones((n, QK_NOPE), F32), cos32, jnp.zeros((n, HEAD_PAD - QK_NOPE - QK_ROPE), F32)], axis=1)
    sin128 = jnp.concatenate([jnp.zeros((n, QK_NOPE), F32), sin32, jnp.zeros((n, HEAD_PAD - QK_NOPE - QK_ROPE), F32)], axis=1)
    return cos32, sin32, cos128, sin128


def _rot_half_cols(w):
    half = QK_ROPE // 2
    return jnp.concatenate([-w[..., half:], w[..., :half]], axis=-1)


def _pad_heads(w, width):
    r = w.shape[0]
    return jnp.pad(w, ((0, 0), (0, 0), (0, HEAD_PAD - width))).reshape(r, HP)


def kernel(x_prompt, x_sample, cache_latent, cache_krope, state_conv, page_table, meta_tokens, a_norm_g, a_pw1_w, a_pw1_b, a_dw_w, a_dw_b, a_ln_g, a_ln_b, a_pw2_w, a_pw2_b, ffn_norm_g, ffn_w_gate, ffn_w_up, ffn_w_down, kv_norm_g, w_dkv, kv_latent_norm_g, w_uk, w_uv, b_norm_g, b_w_dq, b_q_norm_g, b_w_uq, b_w_o, final_norm_g):
    n_a = a_norm_g.shape[0]
    depth = ffn_norm_g.shape[0]
    n_b = depth - n_a
    bp, seq, _ = x_prompt.shape
    t_p = seq + N_META
    bs = x_sample.shape[0]
    past_len = page_table.shape[1] * PAGE_SIZE

    row = lambda v: v.reshape(1, -1).astype(F32)

    pw1_w = a_pw1_w.astype(BF16)
    pw2_w = a_pw2_w.astype(BF16)
    w_gate = ffn_w_gate.astype(BF16)
    w_up = ffn_w_up.astype(BF16)
    w_down = ffn_w_down.astype(BF16)
    dw_pad = jnp.pad(a_dw_w, ((0, 0), (2, 40 - 2 - CONV_W), (0, 0)))
    w_c = w_dkv[:, :KV_LORA].astype(BF16)
    w_kr = w_dkv[:, KV_LORA:]
    w_r = jnp.concatenate([w_kr, _rot_half_cols(w_kr)], axis=1).astype(BF16)
    w_k_pad = _pad_heads(w_uk, QK_NOPE).astype(BF16)
    w_v_pad = _pad_heads(w_uv, V_HEAD).astype(BF16)
    place = jnp.zeros((QK_ROPE, N_HEADS, HEAD_PAD), F32)
    place = place.at[jnp.arange(QK_ROPE), :, QK_NOPE + jnp.arange(QK_ROPE)].set(1.0)
    place = place.reshape(QK_ROPE, HP).astype(BF16)
    w_dq = b_w_dq.astype(BF16)
    w_q_pad = jnp.stack([_pad_heads(b_w_uq[j], QK_NOPE + QK_ROPE) for j in range(n_b)]).astype(BF16)
    w_q_rot = jnp.stack([
        _pad_heads(jnp.concatenate([jnp.zeros((Q_LORA, N_HEADS, QK_NOPE), F32),
                                    _rot_half_cols(b_w_uq[j][..., QK_NOPE:])], axis=-1), QK_NOPE + QK_ROPE)
        for j in range(n_b)]).astype(BF16)
    w_o_pad = jnp.pad(b_w_o.reshape(n_b, N_HEADS, V_HEAD, D_MODEL),
                      ((0, 0), (0, 0), (0, HEAD_PAD - V_HEAD), (0, 0))).reshape(n_b, HP, D_MODEL).astype(BF16)
    w_uk_t = jnp.pad(jnp.transpose(w_uk, (1, 2, 0)), ((0, 0), (0, HEAD_PAD - QK_NOPE), (0, 0))).astype(BF16)
    w_uv_h = jnp.pad(jnp.transpose(w_uv, (1, 0, 2)), ((0, 0), (0, 0), (0, HEAD_PAD - V_HEAD))).astype(BF16)

    def ffn(x, l, tm, attn=None):
        final = row(final_norm_g) if l == depth - 1 else None
        return _ffn(x, row(ffn_norm_g[l]), w_gate[l], w_up[l], w_down[l], tm, attn=attn, final_g=final)

    def conv_args(l):
        return (row(a_dw_b[l]), row(a_ln_g[l]), row(a_ln_b[l]), pw2_w[l], row(a_pw2_b[l]))

    def kv_args():
        return (row(kv_norm_g), w_c, w_r, row(kv_latent_norm_g))

    def q_args(j):
        return (row(b_norm_g[j]), w_dq[j], row(b_q_norm_g[j]), w_q_pad[j], w_q_rot[j])

    xp = jnp.concatenate([jnp.broadcast_to(meta_tokens[None].astype(F32), (bp, N_META, D_MODEL)), x_prompt], axis=1)
    cos32, sin32, cos128, sin128 = _rope_tables(jnp.arange(t_p))
    m_p = bp * t_p
    t_pad = pl.cdiv(t_p, KV_CHUNK) * KV_CHUNK
    x = xp.reshape(m_p, D_MODEL)
    conv_p = []
    for l in range(n_a):
        glu = _pw1_glu(x, row(a_norm_g[l]), pw1_w[l], row(a_pw1_b[l]), ROW_TILE)
        glu3 = glu.reshape(bp, t_p, D_MODEL)
        conv_p.append(glu3[:, t_p - (CONV_W - 1):])
        x = _conv_block(glu3, x.reshape(bp, t_p, D_MODEL), dw_pad[l], *conv_args(l), SEQ_TILE).reshape(m_p, D_MODEL)
        x = ffn(x, l, ROW_TILE)
    c_p, kr_p, k_p, v_p = _kv_side(x.reshape(bp, t_p, D_MODEL), *kv_args(), cos32, sin32, SEQ_TILE,
                                   heads=(w_k_pad, w_v_pad, place), t_pad=t_pad)
    for j in range(n_b):
        q = _q_proj(x.reshape(bp, t_p, D_MODEL), *q_args(j), cos128, sin128, SEQ_TILE)
        o = _attn_prompt(q, k_p, v_p, SEQ_TILE)
        x = ffn(x, n_a + j, ROW_TILE, attn=(o.reshape(m_p, HP), w_o_pad[j]))
    y_prompt = x.reshape(bp, t_p, D_MODEL)[:, N_META:]

    pos_s = jnp.full((bs,), past_len, jnp.int32)
    cos32s, sin32s, cos128s, sin128s = _rope_tables(pos_s)
    x = x_sample.reshape(bs, D_MODEL)
    conv_s = []
    for l in range(n_a):
        glu = _pw1_glu(x, row(a_norm_g[l]), pw1_w[l], row(a_pw1_b[l]), bs)
        conv_s.append(jnp.concatenate([state_conv[l][:, 1:], glu[:, None, :]], axis=1))
        x = _conv_step(state_conv[l], glu, x, a_dw_w[l], *conv_args(l))
        x = ffn(x, l, bs)
    c_s, kr_s = _kv_side(x[None], *kv_args(), cos32s, sin32s, bs)
    c_s = c_s.reshape(bs, 1, KV_LORA)
    kr_s = kr_s.reshape(bs, 1, QK_ROPE)
    for j in range(n_b):
        q = _q_proj(x[None], *q_args(j), cos128s, sin128s, bs)[0]
        q_lat = jnp.transpose(_q_latent(q, w_uk_t), (1, 0, 2))
        q_rope = q.reshape(bs, N_HEADS, HEAD_PAD)[:, :, QK_NOPE:QK_NOPE + QK_ROPE]
        o_lat = _decode_attn(page_table, q_lat, q_rope, c_s, kr_s, cache_latent, cache_krope)
        o = _o_latent_proj(jnp.transpose(o_lat, (1, 0, 2)), w_uv_h)
        x = ffn(x, n_a + j, bs, attn=(o, w_o_pad[j]))
    y_sample = x.reshape(bs, 1, D_MODEL)

    return (y_prompt, y_sample, c_p, kr_p, jnp.stack(conv_p), c_s, kr_s, jnp.stack(conv_s))
```

```python
import functools
import math

import jax
import jax.numpy as jnp
from jax import lax
from jax.experimental import pallas as pl
from jax.experimental.pallas import tpu as pltpu

D_MODEL = 1024
N_HEADS = 16
QK_NOPE = 64
QK_ROPE = 32
V_HEAD = 64
KV_LORA = 256
Q_LORA = 384
CONV_W = 31
N_META = 16
PAGE_SIZE = 128
ROPE_BASE = 10000.0
EPS = 1e-6
SCALE = 1.0 / math.sqrt(QK_NOPE + QK_ROPE)

LANES = 128
HEAD_PAD = LANES
HP = N_HEADS * HEAD_PAD

BF16 = jnp.bfloat16
F32 = jnp.float32

VMEM_LIMIT = 56 * 1024 * 1024

ROW_TILE = 384
SEQ_TILE = 688
Q_TILE = 256
KV_CHUNK = 256
FFN_CHUNK = 256
CONV_STEPS = 16
HIST = 32
DEC_PAGES = 16
DEC_SLOTS = 4


def _params(*sem):
    return pltpu.CompilerParams(dimension_semantics=sem, vmem_limit_bytes=VMEM_LIMIT)


def _const_spec(shape):
    n = len(shape)
    return pl.BlockSpec(shape, lambda *_: (0,) * n, pipeline_mode=pl.Buffered(1))


def _rms(x, g):
    return x * lax.rsqrt(jnp.mean(x * x, axis=-1, keepdims=True) + EPS) * g


def _sigmoid(x):
    return 1.0 / (1.0 + jnp.exp(-x))


def _dot(a, b):
    return jnp.dot(a, b, preferred_element_type=F32)


def _dot_t(a, b):
    return lax.dot_general(a, b, (((1,), (1,)), ((), ())), preferred_element_type=F32)


def _swap8(cols):
    cols = list(cols)
    sub = lax.broadcasted_iota(jnp.int32, cols[0].shape, 1)
    for d in (4, 2, 1):
        keep = (sub & d) == 0
        for j in range(8):
            if j & d:
                continue
            a, b = cols[j], cols[j + d]
            cols[j] = jnp.where(keep, a, pltpu.roll(b, d, axis=1))
            cols[j + d] = jnp.where(keep, pltpu.roll(a, 8 - d, axis=1), b)
    return cols


def _pw1_kernel(x_ref, g_ref, w_ref, b_ref, o_ref, *, tiles):
    h = _rms(x_ref[...], g_ref[...]).astype(BF16)
    u = _dot(h, w_ref[...]) + b_ref[...]
    glu = u[:, :D_MODEL] * _sigmoid(u[:, D_MODEL:])
    if not tiles:
        o_ref[...] = glu
        return
    rows = glu.shape[0]
    cols = _swap8([glu[:, LANES * j:LANES * (j + 1)].reshape(rows // 8, 8, LANES) for j in range(8)])
    for t in range(8):
        o_ref[:, t] = cols[t]


def _pw1_glu(x, g, w, b, tm, tiles):
    m = x.shape[0]
    if tiles:
        out_spec = pl.BlockSpec((tm // 8, 8, 8, LANES), lambda i: (i, 0, 0, 0))
        out_shape = jax.ShapeDtypeStruct((m // 8, 8, 8, LANES), F32)
    else:
        out_spec = pl.BlockSpec((tm, D_MODEL), lambda i: (i, 0))
        out_shape = jax.ShapeDtypeStruct((m, D_MODEL), F32)
    return pl.pallas_call(
        functools.partial(_pw1_kernel, tiles=tiles),
        grid=(m // tm,),
        in_specs=[
            pl.BlockSpec((tm, D_MODEL), lambda i: (i, 0)),
            _const_spec((1, D_MODEL)),
            _const_spec((D_MODEL, 2 * D_MODEL)),
            _const_spec((1, 2 * D_MODEL)),
        ],
        out_specs=out_spec,
        out_shape=out_shape,
        compiler_params=_params("parallel"),
        name="pw1_glu",
    )(x, g, w, b)


def _ln_swish_pw2(y, x, lng_ref, lnb_ref, w2_ref, b2_ref):
    mu = jnp.mean(y, axis=-1, keepdims=True)
    yc = y - mu
    yn = yc * lax.rsqrt(jnp.mean(yc * yc, axis=-1, keepdims=True) + EPS)
    yn = yn * lng_ref[...] + lnb_ref[...]
    z = (yn * _sigmoid(yn)).astype(BF16)
    return x + _dot(z, w2_ref[...]) + b2_ref[...]


def _conv_kernel(g_ref, x_ref, dw_ref, dwb_ref, lng_ref, lnb_ref, w2_ref, b2_ref, o_ref,
                 buf, y_scr, *, tt):
    t = pl.program_id(1)

    @pl.when(t == 0)
    def _():
        buf[0:HIST] = jnp.zeros((HIST, 8, LANES), F32)

    @pl.when(t > 0)
    def _():
        buf[0:HIST] = buf[tt:tt + HIST]

    buf[HIST:HIST + tt] = g_ref[...]
    bias = jnp.broadcast_to(dwb_ref[...], (CONV_STEPS, 8, LANES))

    def block(i, carry):
        rs = i * CONV_STEPS
        acc = bias
        for o in range(HIST - (CONV_W - 1), HIST + 1):
            acc = acc + buf[pl.ds(rs + o, CONV_STEPS)] * dw_ref[o]
        y_scr[pl.ds(i * (CONV_STEPS // 8), CONV_STEPS // 8)] = acc.reshape(CONV_STEPS // 8, 8, 8, LANES)
        return carry

    lax.fori_loop(0, tt // CONV_STEPS, block, 0)
    cols = _swap8([y_scr[:, s] for s in range(8)])
    y = jnp.concatenate([c.reshape(tt, LANES) for c in cols], axis=1)
    o_ref[...] = _ln_swish_pw2(y, x_ref[...], lng_ref, lnb_ref, w2_ref, b2_ref)


def _conv_block(glu, x, dw_tiles, dwb_tile, lng, lnb, w2, b2, tt):
    b, t, _ = x.shape
    assert tt % CONV_STEPS == 0 and t % tt == 0
    rows = pl.BlockSpec((None, tt, D_MODEL), lambda bi, ti: (bi, ti, 0))
    return pl.pallas_call(
        functools.partial(_conv_kernel, tt=tt),
        grid=(b, t // tt),
        in_specs=[
            pl.BlockSpec((None, tt, 8, LANES), lambda bi, ti: (bi, ti, 0, 0)),
            rows,
            _const_spec((HIST + 8, 8, LANES)),
            _const_spec((8, LANES)),
            _const_spec((1, D_MODEL)),
            _const_spec((1, D_MODEL)),
            _const_spec((D_MODEL, D_MODEL)),
            _const_spec((1, D_MODEL)),
        ],
        out_specs=rows,
        out_shape=jax.ShapeDtypeStruct((b, t, D_MODEL), F32),
        scratch_shapes=[
            pltpu.VMEM((HIST + tt, 8, LANES), F32),
            pltpu.VMEM((tt // 8, 8, 8, LANES), F32),
        ],
        compiler_params=_params("parallel", "arbitrary"),
        name="conv_block",
    )(glu, x, dw_tiles, dwb_tile, lng, lnb, w2, b2)


def _conv_step_kernel(st_ref, g_ref, x_ref, dw_ref, dwb_ref, lng_ref, lnb_ref, w2_ref, b2_ref, o_ref):
    g = g_ref[...]
    y = g * dw_ref[CONV_W - 1:CONV_W, :] + dwb_ref[...]
    for k in range(CONV_W - 1):
        y = y + st_ref[k] * dw_ref[k:k + 1, :]
    o_ref[...] = _ln_swish_pw2(y, x_ref[...], lng_ref, lnb_ref, w2_ref, b2_ref)


def _conv_step(state, glu, x, dw, dwb, lng, lnb, w2, b2):
    m = x.shape[0]
    return pl.pallas_call(
        _conv_step_kernel,
        out_shape=jax.ShapeDtypeStruct((m, D_MODEL), F32),
        compiler_params=pltpu.CompilerParams(vmem_limit_bytes=VMEM_LIMIT),
        name="conv_step",
    )(state, glu, x, dw, dwb, lng, lnb, w2, b2)


def _ffn_kernel(*refs, pre, final, d_ff):
    it = iter(refs)
    x_ref = next(it)
    if pre:
        o_ref, wo_ref = next(it), next(it)
    g_ref, wg_ref, wu_ref, wd_ref = next(it), next(it), next(it), next(it)
    if final:
        gf_ref = next(it)
    out_ref, a_scr = next(it), next(it)

    x = x_ref[...]
    if pre:
        x = x + _dot(o_ref[...], wo_ref[...])
    h = _rms(x, g_ref[...]).astype(BF16)
    for c in range(d_ff // FFN_CHUNK):
        sl = slice(c * FFN_CHUNK, (c + 1) * FFN_CHUNK)
        gate = _dot(h, wg_ref[:, sl])
        up = _dot(h, wu_ref[:, sl])
        a_scr[:, sl] = (gate * _sigmoid(gate) * up).astype(BF16)
    y = x + _dot(a_scr[...], wd_ref[...])
    if final:
        y = _rms(y, gf_ref[...])
    out_ref[...] = y


def _ffn(x, g, wg, wu, wd, tm, attn=None, final_g=None):
    m = x.shape[0]
    d_ff = wg.shape[1]
    row = lambda i: (i, 0)
    args = [x]
    specs = [pl.BlockSpec((tm, D_MODEL), row)]
    if attn is not None:
        o, wo = attn
        args += [o, wo]
        specs += [pl.BlockSpec((tm, HP), row), _const_spec((HP, D_MODEL))]
    args += [g, wg, wu, wd]
    specs += [_const_spec((1, D_MODEL)), _const_spec((D_MODEL, d_ff)),
              _const_spec((D_MODEL, d_ff)), _const_spec((d_ff, D_MODEL))]
    if final_g is not None:
        args.append(final_g)
        specs.append(_const_spec((1, D_MODEL)))
    return pl.pallas_call(
        functools.partial(_ffn_kernel, pre=attn is not None, final=final_g is not None, d_ff=d_ff),
        grid=(m // tm,),
        in_specs=specs,
        out_specs=pl.BlockSpec((tm, D_MODEL), row),
        out_shape=jax.ShapeDtypeStruct((m, D_MODEL), F32),
        scratch_shapes=[pltpu.VMEM((tm, d_ff), BF16)],
        compiler_params=_params("parallel"),
        name="ffn",
    )(*args)


def _kv_kernel(*refs, heads, n_real):
    if heads:
        (x_ref, g_ref, wc_ref, wr_ref, gl_ref, cos_ref, sin_ref, wk_ref, wv_ref, e_ref,
         c_ref, kr_ref, k_ref, v_ref) = refs
    else:
        x_ref, g_ref, wc_ref, wr_ref, gl_ref, cos_ref, sin_ref, c_ref, kr_ref = refs

    def real():
        h = _rms(x_ref[...], g_ref[...]).astype(BF16)
        c = _rms(_dot(h, wc_ref[...]), gl_ref[...])
        r = _dot(h, wr_ref[...])
        kr = r[:, :QK_ROPE] * cos_ref[...] + r[:, QK_ROPE:] * sin_ref[...]
        c_ref[...] = c
        kr_ref[...] = kr
        if heads:
            cb = c.astype(BF16)
            k_ref[...] = (_dot(cb, wk_ref[...]) + _dot(kr.astype(BF16), e_ref[...])).astype(BF16)
            v_ref[...] = _dot(cb, wv_ref[...]).astype(BF16)

    if not heads:
        real()
        return

    t = pl.program_id(1)
    pl.when(t < n_real)(real)

    @pl.when(t >= n_real)
    def _():
        k_ref[...] = jnp.zeros_like(k_ref)
        v_ref[...] = jnp.zeros_like(v_ref)


def _kv_side(x, g, wc, wr, gl, cos, sin, tt, heads=None, t_pad=None):
    b, t, _ = x.shape
    n_real = t // tt
    n_steps = n_real if heads is None else pl.cdiv(t_pad, tt)
    clamp = lambda bi, ti: (bi, jnp.minimum(ti, n_real - 1), 0)
    tab = lambda bi, ti: (jnp.minimum(ti, n_real - 1), 0)
    args = [x, g, wc, wr, gl, cos, sin]
    specs = [
        pl.BlockSpec((None, tt, D_MODEL), clamp),
        _const_spec((1, D_MODEL)),
        _const_spec((D_MODEL, KV_LORA)),
        _const_spec((D_MODEL, 2 * QK_ROPE)),
        _const_spec((1, KV_LORA)),
        pl.BlockSpec((tt, QK_ROPE), tab),
        pl.BlockSpec((tt, QK_ROPE), tab),
    ]
    out_specs = [pl.BlockSpec((None, tt, KV_LORA), clamp), pl.BlockSpec((None, tt, QK_ROPE), clamp)]
    out_shape = [jax.ShapeDtypeStruct((b, t, KV_LORA), F32), jax.ShapeDtypeStruct((b, t, QK_ROPE), F32)]
    if heads is not None:
        args += list(heads)
        specs += [_const_spec((KV_LORA, HP)), _const_spec((KV_LORA, HP)), _const_spec((QK_ROPE, HP))]
        full = lambda bi, ti: (bi, ti, 0)
        out_specs += [pl.BlockSpec((None, tt, HP), full)] * 2
        out_shape += [jax.ShapeDtypeStruct((b, t_pad, HP), BF16)] * 2
    return pl.pallas_call(
        functools.partial(_kv_kernel, heads=heads is not None, n_real=n_real),
        grid=(b, n_steps),
        in_specs=specs,
        out_specs=out_specs,
        out_shape=out_shape,
        compiler_params=_params("parallel", "arbitrary"),
        name="kv_side",
    )(*args)


def _q_kernel(x_ref, g_ref, wdq_ref, gq_ref, wq_ref, wqr_ref, cos_ref, sin_ref, q_ref):
    h = _rms(x_ref[...], g_ref[...]).astype(BF16)
    cq = _rms(_dot(h, wdq_ref[...]), gq_ref[...]).astype(BF16)
    q = _dot(cq, wq_ref[...])
    qr = _dot(cq, wqr_ref[...])
    cos = cos_ref[...]
    sin = sin_ref[...]
    for hd in range(N_HEADS):
        sl = slice(hd * HEAD_PAD, (hd + 1) * HEAD_PAD)
        q_ref[:, sl] = ((q[:, sl] * cos + qr[:, sl] * sin) * SCALE).astype(BF16)


def _q_proj(x, g, wdq, gq, wq, wqr, cos, sin, tt):
    b, t, _ = x.shape
    full = lambda bi, ti: (bi, ti, 0)
    tab = lambda bi, ti: (ti, 0)
    return pl.pallas_call(
        _q_kernel,
        grid=(b, t // tt),
        in_specs=[
            pl.BlockSpec((None, tt, D_MODEL), full),
            _const_spec((1, D_MODEL)),
            _const_spec((D_MODEL, Q_LORA)),
            _const_spec((1, Q_LORA)),
            _const_spec((Q_LORA, HP)),
            _const_spec((Q_LORA, HP)),
            pl.BlockSpec((tt, HEAD_PAD), tab),
            pl.BlockSpec((tt, HEAD_PAD), tab),
        ],
        out_specs=pl.BlockSpec((None, tt, HP), full),
        out_shape=jax.ShapeDtypeStruct((b, t, HP), BF16),
        compiler_params=_params("parallel", "parallel"),
        name="q_proj",
    )(x, g, wdq, gq, wq, wqr, cos, sin)


def _attn_kernel(q_ref, k_ref, v_ref, o_ref, s_scr, p_scr, *, t):
    half = KV_CHUNK // 2
    tiles = [(q0, min(Q_TILE, t - q0)) for q0 in range(0, t, Q_TILE)]
    for idx, (q0, rows) in enumerate(tiles):
        s_buf = s_scr.at[idx % 2]
        p_buf = p_scr.at[idx % 2]
        n_chunks = pl.cdiv(q0 + rows, KV_CHUNK)
        q = q_ref[q0:q0 + rows, :]
        mx = None
        for c in range(n_chunks):
            ks = c * KV_CHUNK
            s = _dot_t(q, k_ref[ks:ks + KV_CHUNK, :])
            if ks + KV_CHUNK - 1 > q0:
                qpos = q0 + lax.broadcasted_iota(jnp.int32, s.shape, 0)
                kpos = ks + lax.broadcasted_iota(jnp.int32, s.shape, 1)
                s = jnp.where(kpos <= qpos, s, -jnp.inf)
            s_buf[0:rows, ks:ks + KV_CHUNK] = s
            cm = jnp.maximum(s[:, :half], s[:, half:])
            mx = cm if mx is None else jnp.maximum(mx, cm)
        m = jnp.max(mx, axis=1, keepdims=True)
        sm = None
        for c in range(n_chunks):
            ks = c * KV_CHUNK
            p = jnp.exp(s_buf[0:rows, ks:ks + KV_CHUNK] - m)
            p_buf[0:rows, ks:ks + KV_CHUNK] = p.astype(BF16)
            ps = p[:, :half] + p[:, half:]
            sm = ps if sm is None else sm + ps
        l = jnp.sum(sm, axis=1, keepdims=True)
        n_keys = n_chunks * KV_CHUNK
        o = _dot(p_buf[0:rows, 0:n_keys], v_ref[0:n_keys, :])
        o_ref[q0:q0 + rows, :] = (o / l).astype(BF16)


def _attn_prompt(q, k, v):
    b, t, _ = q.shape
    t_pad = k.shape[1]
    qo = pl.BlockSpec((None, t, HEAD_PAD), lambda bi, hi: (bi, 0, hi))
    kv = pl.BlockSpec((None, t_pad, HEAD_PAD), lambda bi, hi: (bi, 0, hi))
    return pl.pallas_call(
        functools.partial(_attn_kernel, t=t),
        grid=(b, N_HEADS),
        in_specs=[qo, kv, kv],
        out_specs=qo,
        out_shape=jax.ShapeDtypeStruct((b, t, HP), BF16),
        scratch_shapes=[pltpu.VMEM((2, Q_TILE, t_pad), F32), pltpu.VMEM((2, Q_TILE, t_pad), BF16)],
        compiler_params=_params("parallel", "parallel"),
        name="attn_prompt",
    )(q, k, v)


def _qlat_kernel(q_ref, w_ref, o_ref):
    o_ref[...] = _dot(q_ref[...], w_ref[...]).astype(BF16)


def _q_latent(q, w_uk_t):
    m = q.shape[0]
    return pl.pallas_call(
        _qlat_kernel,
        grid=(N_HEADS,),
        in_specs=[pl.BlockSpec((m, HEAD_PAD), lambda h: (0, h)),
                  pl.BlockSpec((None, HEAD_PAD, KV_LORA), lambda h: (h, 0, 0))],
        out_specs=pl.BlockSpec((None, m, KV_LORA), lambda h: (h, 0, 0)),
        out_shape=jax.ShapeDtypeStruct((N_HEADS, m, KV_LORA), BF16),
        compiler_params=_params("parallel"),
        name="q_latent",
    )(q, w_uk_t)


def _decode_kernel(pt_ref, ql_ref, qr_ref, cn_ref, krn_ref, lat_hbm, krt_hbm, o_ref,
                   lat_buf, kr_buf, lat_sem, kr_sem, m_scr, l_scr, acc_scr, *, groups):
    n_seq = ql_ref.shape[0]
    n_iter = n_seq * groups // 2
    lanes = lambda a, n: jnp.concatenate([a] * n, axis=1)

    def copies(t, slot):
        b = t // groups
        first_page = (t % groups) * DEC_PAGES
        out = []
        for j in range(DEC_PAGES):
            page = pt_ref[b, first_page + j]
            out.append(pltpu.make_async_copy(
                lat_hbm.at[page], lat_buf.at[slot, pl.ds(j * PAGE_SIZE, PAGE_SIZE)], lat_sem.at[slot]))
            out.append(pltpu.make_async_copy(krt_hbm.at[page], kr_buf.at[slot, j], kr_sem.at[slot]))
        return out

    def start(t, slot):
        for cp in copies(t, slot):
            cp.start()

    def wait(t, slot):
        for cp in copies(t, slot):
            cp.wait()

    def group(slot, ql, qr):
        c = lat_buf[slot].astype(BF16)
        s_rope = jnp.concatenate(
            [_dot(qr, kr_buf[slot, j].astype(BF16)) for j in range(DEC_PAGES)], axis=1)
        s = _dot_t(ql, c) + s_rope
        m = jnp.max(s, axis=1, keepdims=True)
        p = jnp.exp(s - m)
        return m, jnp.sum(p, axis=1, keepdims=True), _dot(p.astype(BF16), c)

    start(0, 0)
    start(1, 1)

    def body(i, carry):
        base = (i % 2) * 2
        b = (2 * i) // groups
        ql = ql_ref[b]
        qr = qr_ref[b]

        @pl.when(i + 1 < n_iter)
        def _():
            start(2 * i + 2, 2 - base)
            start(2 * i + 3, 3 - base)

        @pl.when((2 * i) % groups == 0)
        def _():
            cn = cn_ref[b].astype(BF16).astype(F32)
            krn = krn_ref[b].astype(BF16).astype(F32)
            s_new = (jnp.sum(ql.astype(F32) * cn, axis=1, keepdims=True)
                     + jnp.sum(qr.astype(F32) * krn, axis=1, keepdims=True))
            m_scr[...] = jnp.broadcast_to(s_new, m_scr.shape)
            l_scr[...] = jnp.ones_like(l_scr)
            acc_scr[...] = jnp.broadcast_to(cn, acc_scr.shape)

        wait(2 * i, base)
        m0, l0, o0 = group(base, ql, qr)
        wait(2 * i + 1, base + 1)
        m1, l1, o1 = group(base + 1, ql, qr)

        m_prev = m_scr[...]
        m_new = jnp.maximum(m_prev, jnp.maximum(m0, m1))
        a_prev = jnp.exp(m_prev - m_new)
        a0 = jnp.exp(m0 - m_new)
        a1 = jnp.exp(m1 - m_new)
        l_scr[...] = a_prev * l_scr[...] + a0 * l0 + a1 * l1
        acc_scr[...] = lanes(a_prev, 2) * acc_scr[...] + lanes(a0, 2) * o0 + lanes(a1, 2) * o1
        m_scr[...] = m_new

        @pl.when((2 * i + 2) % groups == 0)
        def _():
            o_ref[b] = acc_scr[...] / lanes(l_scr[...], 2)

        return carry

    lax.fori_loop(0, n_iter, body, 0)


def _decode_attn(page_table, q_lat, q_rope, c_new, kr_new, cache_latent, cache_krope_t):
    b, n_pages = page_table.shape
    groups = n_pages // DEC_PAGES
    assert n_pages % DEC_PAGES == 0 and groups % 2 == 0
    vmem = pl.BlockSpec(memory_space=pltpu.VMEM)
    return pl.pallas_call(
        functools.partial(_decode_kernel, groups=groups),
        in_specs=[pl.BlockSpec(memory_space=pltpu.SMEM), vmem, vmem, vmem, vmem,
                  pl.BlockSpec(memory_space=pl.ANY), pl.BlockSpec(memory_space=pl.ANY)],
        out_specs=vmem,
        out_shape=jax.ShapeDtypeStruct((b, N_HEADS, KV_LORA), F32),
        scratch_shapes=[
            pltpu.VMEM((DEC_SLOTS, DEC_PAGES * PAGE_SIZE, KV_LORA), F32),
            pltpu.VMEM((DEC_SLOTS, DEC_PAGES, QK_ROPE, PAGE_SIZE), F32),
            pltpu.SemaphoreType.DMA((DEC_SLOTS,)),
            pltpu.SemaphoreType.DMA((DEC_SLOTS,)),
            pltpu.VMEM((N_HEADS, HEAD_PAD), F32),
            pltpu.VMEM((N_HEADS, HEAD_PAD), F32),
            pltpu.VMEM((N_HEADS, KV_LORA), F32),
        ],
        compiler_params=pltpu.CompilerParams(vmem_limit_bytes=VMEM_LIMIT),
        name="decode_attn",
    )(page_table, q_lat, q_rope, c_new, kr_new, cache_latent, cache_krope_t)


def _olat_kernel(o_ref, w_ref, out_ref):
    out_ref[...] = _dot(o_ref[...].astype(BF16), w_ref[...]).astype(BF16)


def _o_latent_proj(o_lat, w_uv_pad):
    m = o_lat.shape[1]
    return pl.pallas_call(
        _olat_kernel,
        grid=(N_HEADS,),
        in_specs=[pl.BlockSpec((None, m, KV_LORA), lambda h: (h, 0, 0)),
                  pl.BlockSpec((None, KV_LORA, HEAD_PAD), lambda h: (h, 0, 0))],
        out_specs=pl.BlockSpec((m, HEAD_PAD), lambda h: (0, h)),
        out_shape=jax.ShapeDtypeStruct((m, HP), BF16),
        compiler_params=_params("parallel"),
        name="o_latent_proj",
    )(o_lat, w_uv_pad)


def _rope_tables(pos):
    half = QK_ROPE // 2
    inv = jnp.power(ROPE_BASE, -jnp.arange(half, dtype=F32) / half)
    ang = pos.astype(F32)[:, None] * inv[None, :]
    cos = jnp.cos(ang)
    sin = jnp.sin(ang)
    cos32 = jnp.concatenate([cos, cos], axis=1)
    sin32 = jnp.concatenate([sin, sin], axis=1)
    n = pos.shape[0]
    cos128 = jnp.concatenate([jnp.ones((n, QK_NOPE), F32), cos32, jnp.zeros((n, HEAD_PAD - QK_NOPE - QK_ROPE), F32)], axis=1)
    sin128 = jnp.concatenate([jnp.zeros((n, QK_NOPE), F32), sin32, jnp.zeros((n, HEAD_PAD - QK_NOPE - QK_ROPE), F32)], axis=1)
    return cos32, sin32, cos128, sin128


def _rot_half_cols(w):
    half = QK_ROPE // 2
    return jnp.concatenate([-w[..., half:], w[..., :half]], axis=-1)


def _pad_heads(w, width):
    r = w.shape[0]
    return jnp.pad(w, ((0, 0), (0, 0), (0, HEAD_PAD - width))).reshape(r, HP)


def kernel(x_prompt, x_sample, cache_latent, cache_krope, state_conv, page_table, meta_tokens, a_norm_g, a_pw1_w, a_pw1_b, a_dw_w, a_dw_b, a_ln_g, a_ln_b, a_pw2_w, a_pw2_b, ffn_norm_g, ffn_w_gate, ffn_w_up, ffn_w_down, kv_norm_g, w_dkv, kv_latent_norm_g, w_uk, w_uv, b_norm_g, b_w_dq, b_q_norm_g, b_w_uq, b_w_o, final_norm_g):
    n_a = a_norm_g.shape[0]
    depth = ffn_norm_g.shape[0]
    n_b = depth - n_a
    bp, seq, _ = x_prompt.shape
    t_p = seq + N_META
    bs = x_sample.shape[0]
    past_len = page_table.shape[1] * PAGE_SIZE

    row = lambda v: v.reshape(1, -1).astype(F32)

    pw1_w = a_pw1_w.astype(BF16)
    pw2_w = a_pw2_w.astype(BF16)
    w_gate = ffn_w_gate.astype(BF16)
    w_up = ffn_w_up.astype(BF16)
    w_down = ffn_w_down.astype(BF16)
    first_tap = HIST - (CONV_W - 1)
    dw_tiles = jnp.pad(a_dw_w, ((0, 0), (first_tap, HIST + 8 - first_tap - CONV_W), (0, 0)))
    dw_tiles = dw_tiles.reshape(n_a, HIST + 8, 8, LANES)
    w_c = w_dkv[:, :KV_LORA].astype(BF16)
    w_kr = w_dkv[:, KV_LORA:]
    w_r = jnp.concatenate([w_kr, _rot_half_cols(w_kr)], axis=1).astype(BF16)
    w_k_pad = _pad_heads(w_uk, QK_NOPE).astype(BF16)
    w_v_pad = _pad_heads(w_uv, V_HEAD).astype(BF16)
    place = jnp.zeros((QK_ROPE, N_HEADS, HEAD_PAD), F32)
    place = place.at[jnp.arange(QK_ROPE), :, QK_NOPE + jnp.arange(QK_ROPE)].set(1.0)
    place = place.reshape(QK_ROPE, HP).astype(BF16)
    w_dq = b_w_dq.astype(BF16)
    w_q_pad = jnp.stack([_pad_heads(b_w_uq[j], QK_NOPE + QK_ROPE) for j in range(n_b)]).astype(BF16)
    w_q_rot = jnp.stack([
        _pad_heads(jnp.concatenate([jnp.zeros((Q_LORA, N_HEADS, QK_NOPE), F32),
                                    _rot_half_cols(b_w_uq[j][..., QK_NOPE:])], axis=-1), QK_NOPE + QK_ROPE)
        for j in range(n_b)]).astype(BF16)
    w_o_pad = jnp.pad(b_w_o.reshape(n_b, N_HEADS, V_HEAD, D_MODEL),
                      ((0, 0), (0, 0), (0, HEAD_PAD - V_HEAD), (0, 0))).reshape(n_b, HP, D_MODEL).astype(BF16)
    w_uk_t = jnp.pad(jnp.transpose(w_uk, (1, 2, 0)), ((0, 0), (0, HEAD_PAD - QK_NOPE), (0, 0))).astype(BF16)
    w_uv_h = jnp.pad(jnp.transpose(w_uv, (1, 0, 2)), ((0, 0), (0, 0), (0, HEAD_PAD - V_HEAD))).astype(BF16)

    def ffn(x, l, tm, attn=None):
        final = row(final_norm_g) if l == depth - 1 else None
        return _ffn(x, row(ffn_norm_g[l]), w_gate[l], w_up[l], w_down[l], tm, attn=attn, final_g=final)

    def conv_args(l):
        return (row(a_ln_g[l]), row(a_ln_b[l]), pw2_w[l], row(a_pw2_b[l]))

    def kv_args():
        return (row(kv_norm_g), w_c, w_r, row(kv_latent_norm_g))

    def q_args(j):
        return (row(b_norm_g[j]), w_dq[j], row(b_q_norm_g[j]), w_q_pad[j], w_q_rot[j])

    xp = jnp.concatenate([jnp.broadcast_to(meta_tokens[None].astype(F32), (bp, N_META, D_MODEL)), x_prompt], axis=1)
    cos32, sin32, cos128, sin128 = _rope_tables(jnp.arange(t_p))
    m_p = bp * t_p
    t_pad = pl.cdiv(t_p, KV_CHUNK) * KV_CHUNK
    x = xp.reshape(m_p, D_MODEL)
    conv_p = []
    for l in range(n_a):
        glu = _pw1_glu(x, row(a_norm_g[l]), pw1_w[l], row(a_pw1_b[l]), ROW_TILE, tiles=True)
        glu = glu.reshape(bp, t_p, 8, LANES)
        conv_p.append(glu[:, t_p - (CONV_W - 1):].reshape(bp, CONV_W - 1, D_MODEL))
        x = _conv_block(glu, x.reshape(bp, t_p, D_MODEL), dw_tiles[l], a_dw_b[l].reshape(8, LANES),
                        *conv_args(l), SEQ_TILE).reshape(m_p, D_MODEL)
        x = ffn(x, l, ROW_TILE)
    c_p, kr_p, k_p, v_p = _kv_side(x.reshape(bp, t_p, D_MODEL), *kv_args(), cos32, sin32, SEQ_TILE,
                                   heads=(w_k_pad, w_v_pad, place), t_pad=t_pad)
    for j in range(n_b):
        q = _q_proj(x.reshape(bp, t_p, D_MODEL), *q_args(j), cos128, sin128, SEQ_TILE)
        o = _attn_prompt(q, k_p, v_p)
        x = ffn(x, n_a + j, ROW_TILE, attn=(o.reshape(m_p, HP), w_o_pad[j]))
    y_prompt = x.reshape(bp, t_p, D_MODEL)[:, N_META:]

    pos_s = jnp.full((bs,), past_len, jnp.int32)
    cos32s, sin32s, cos128s, sin128s = _rope_tables(pos_s)
    x = x_sample.reshape(bs, D_MODEL)
    conv_s = []
    state_t = jnp.transpose(state_conv, (0, 2, 1, 3))
    krope_t = jnp.swapaxes(cache_krope, 1, 2)
    for l in range(n_a):
        glu = _pw1_glu(x, row(a_norm_g[l]), pw1_w[l], row(a_pw1_b[l]), bs, tiles=False)
        conv_s.append(jnp.concatenate([state_conv[l][:, 1:], glu[:, None, :]], axis=1))
        x = _conv_step(state_t[l], glu, x, a_dw_w[l], row(a_dw_b[l]), *conv_args(l))
        x = ffn(x, l, bs)
    c_s, kr_s = _kv_side(x[None], *kv_args(), cos32s, sin32s, bs)
    c_s = c_s.reshape(bs, 1, KV_LORA)
    kr_s = kr_s.reshape(bs, 1, QK_ROPE)
    for j in range(n_b):
        q = _q_proj(x[None], *q_args(j), cos128s, sin128s, bs)[0]
        q_lat = jnp.transpose(_q_latent(q, w_uk_t), (1, 0, 2))
        q_rope = q.reshape(bs, N_HEADS, HEAD_PAD)[:, :, QK_NOPE:QK_NOPE + QK_ROPE]
        o_lat = _decode_attn(page_table, q_lat, q_rope, c_s, kr_s, cache_latent, krope_t)
        o = _o_latent_proj(jnp.transpose(o_lat, (1, 0, 2)), w_uv_h)
        x = ffn(x, n_a + j, bs, attn=(o, w_o_pad[j]))
    y_sample = x.reshape(bs, 1, D_MODEL)

    return (y_prompt, y_sample, c_p, kr_p, jnp.stack(conv_p), c_s, kr_s, jnp.stack(conv_s))
```

```python
import functools
import math

import jax
import jax.numpy as jnp
from jax import lax
from jax.experimental import pallas as pl
from jax.experimental.pallas import tpu as pltpu

D_MODEL = 1024
N_HEADS = 16
QK_NOPE = 64
QK_ROPE = 32
V_HEAD = 64
KV_LORA = 256
Q_LORA = 384
CONV_W = 31
N_META = 16
PAGE_SIZE = 128
ROPE_BASE = 10000.0
EPS = 1e-6
SCALE = 1.0 / math.sqrt(QK_NOPE + QK_ROPE)
Q_SCALE = SCALE * math.log2(math.e)

LANES = 128
HEAD_PAD = LANES
HP = N_HEADS * HEAD_PAD

BF16 = jnp.bfloat16
F32 = jnp.float32

VMEM_LIMIT = 56 * 1024 * 1024

ROW_TILE = 384
SEQ_TILE = 688
OUT_TILE = 512
Q_TILE = 256
KV_CHUNK = 256
FFN_CHUNK = 256
CONV_STEPS = 16
HIST = 32
DEC_PAGES = 32
DEC_SLOTS = 4


def _params(*sem):
    return pltpu.CompilerParams(dimension_semantics=sem, vmem_limit_bytes=VMEM_LIMIT)


def _const_spec(shape):
    n = len(shape)
    return pl.BlockSpec(shape, lambda *_: (0,) * n, pipeline_mode=pl.Buffered(1))


def _rms(x, g):
    return x * lax.rsqrt(jnp.mean(x * x, axis=-1, keepdims=True) + EPS) * g


def _sigmoid(x):
    return 1.0 / (1.0 + jnp.exp(-x))


def _dot(a, b):
    return jnp.dot(a, b, preferred_element_type=F32)


def _dot_t(a, b):
    return lax.dot_general(a, b, (((1,), (1,)), ((), ())), preferred_element_type=F32)


def _swap8(cols):
    cols = list(cols)
    sub = lax.broadcasted_iota(jnp.int32, cols[0].shape, 1)
    for d in (4, 2, 1):
        keep = (sub & d) == 0
        for j in range(8):
            if j & d:
                continue
            a, b = cols[j], cols[j + d]
            cols[j] = jnp.where(keep, a, pltpu.roll(b, d, axis=1))
            cols[j + d] = jnp.where(keep, pltpu.roll(a, 8 - d, axis=1), b)
    return cols


def _pw1_kernel(x_ref, g_ref, w_ref, b_ref, o_ref, *, tiles):
    h = _rms(x_ref[...], g_ref[...]).astype(BF16)
    u = _dot(h, w_ref[...]) + b_ref[...]
    glu = u[:, :D_MODEL] * _sigmoid(u[:, D_MODEL:])
    if not tiles:
        o_ref[...] = glu
        return
    rows = glu.shape[0]
    cols = _swap8([glu[:, LANES * j:LANES * (j + 1)].reshape(rows // 8, 8, LANES) for j in range(8)])
    for t in range(8):
        o_ref[:, t] = cols[t]


def _pw1_glu(x, g, w, b, tm, tiles):
    m = x.shape[0]
    if tiles:
        out_spec = pl.BlockSpec((tm // 8, 8, 8, LANES), lambda i: (i, 0, 0, 0))
        out_shape = jax.ShapeDtypeStruct((m // 8, 8, 8, LANES), F32)
    else:
        out_spec = pl.BlockSpec((tm, D_MODEL), lambda i: (i, 0))
        out_shape = jax.ShapeDtypeStruct((m, D_MODEL), F32)
    return pl.pallas_call(
        functools.partial(_pw1_kernel, tiles=tiles),
        grid=(m // tm,),
        in_specs=[
            pl.BlockSpec((tm, D_MODEL), lambda i: (i, 0)),
            _const_spec((1, D_MODEL)),
            _const_spec((D_MODEL, 2 * D_MODEL)),
            _const_spec((1, 2 * D_MODEL)),
        ],
        out_specs=out_spec,
        out_shape=out_shape,
        compiler_params=_params("parallel"),
        name="pw1_glu",
    )(x, g, w, b)


def _ln_swish_pw2(y, x, lng_ref, lnb_ref, w2_ref, b2_ref):
    mu = jnp.mean(y, axis=-1, keepdims=True)
    yc = y - mu
    yn = yc * lax.rsqrt(jnp.mean(yc * yc, axis=-1, keepdims=True) + EPS)
    yn = yn * lng_ref[...] + lnb_ref[...]
    z = (yn * _sigmoid(yn)).astype(BF16)
    return x + _dot(z, w2_ref[...]) + b2_ref[...]


def _conv_kernel(g_ref, x_ref, dw_ref, dwb_ref, lng_ref, lnb_ref, w2_ref, b2_ref, o_ref,
                 buf, y_scr, *, tt):
    t = pl.program_id(1)

    @pl.when(t == 0)
    def _():
        buf[0:HIST] = jnp.zeros((HIST, 8, LANES), F32)

    @pl.when(t > 0)
    def _():
        buf[0:HIST] = buf[tt:tt + HIST]

    buf[HIST:HIST + tt] = g_ref[...]
    bias = jnp.broadcast_to(dwb_ref[...], (CONV_STEPS, 8, LANES))

    def block(i, carry):
        rs = i * CONV_STEPS
        acc = bias
        for o in range(HIST - (CONV_W - 1), HIST + 1):
            acc = acc + buf[pl.ds(rs + o, CONV_STEPS)] * dw_ref[o]
        y_scr[pl.ds(i * (CONV_STEPS // 8), CONV_STEPS // 8)] = acc.reshape(CONV_STEPS // 8, 8, 8, LANES)
        return carry

    lax.fori_loop(0, tt // CONV_STEPS, block, 0)
    cols = _swap8([y_scr[:, s] for s in range(8)])
    y = jnp.concatenate([c.reshape(tt, LANES) for c in cols], axis=1)
    o_ref[...] = _ln_swish_pw2(y, x_ref[...], lng_ref, lnb_ref, w2_ref, b2_ref)


def _conv_block(glu, x, dw_tiles, dwb_tile, lng, lnb, w2, b2, tt):
    b, t, _ = x.shape
    assert tt % CONV_STEPS == 0 and t % tt == 0
    rows = pl.BlockSpec((None, tt, D_MODEL), lambda bi, ti: (bi, ti, 0))
    return pl.pallas_call(
        functools.partial(_conv_kernel, tt=tt),
        grid=(b, t // tt),
        in_specs=[
            pl.BlockSpec((None, tt, 8, LANES), lambda bi, ti: (bi, ti, 0, 0)),
            rows,
            _const_spec((HIST + 8, 8, LANES)),
            _const_spec((8, LANES)),
            _const_spec((1, D_MODEL)),
            _const_spec((1, D_MODEL)),
            _const_spec((D_MODEL, D_MODEL)),
            _const_spec((1, D_MODEL)),
        ],
        out_specs=rows,
        out_shape=jax.ShapeDtypeStruct((b, t, D_MODEL), F32),
        scratch_shapes=[
            pltpu.VMEM((HIST + tt, 8, LANES), F32),
            pltpu.VMEM((tt // 8, 8, 8, LANES), F32),
        ],
        compiler_params=_params("parallel", "arbitrary"),
        name="conv_block",
    )(glu, x, dw_tiles, dwb_tile, lng, lnb, w2, b2)


def _conv_step_kernel(st_ref, g_ref, x_ref, dw_ref, dwb_ref, lng_ref, lnb_ref, w2_ref, b2_ref, o_ref):
    g = g_ref[...]
    y = g * dw_ref[CONV_W - 1:CONV_W, :] + dwb_ref[...]
    for k in range(CONV_W - 1):
        y = y + st_ref[k] * dw_ref[k:k + 1, :]
    o_ref[...] = _ln_swish_pw2(y, x_ref[...], lng_ref, lnb_ref, w2_ref, b2_ref)


def _conv_step(state, glu, x, dw, dwb, lng, lnb, w2, b2):
    m = x.shape[0]
    return pl.pallas_call(
        _conv_step_kernel,
        out_shape=jax.ShapeDtypeStruct((m, D_MODEL), F32),
        compiler_params=pltpu.CompilerParams(vmem_limit_bytes=VMEM_LIMIT),
        name="conv_step",
    )(state, glu, x, dw, dwb, lng, lnb, w2, b2)


def _ffn_kernel(*refs, pre, final, d_ff):
    it = iter(refs)
    x_ref = next(it)
    if pre:
        o_ref, wo_ref = next(it), next(it)
    g_ref, wg_ref, wu_ref, wd_ref = next(it), next(it), next(it), next(it)
    if final:
        gf_ref = next(it)
    out_ref, a_scr = next(it), next(it)

    x = x_ref[...]
    if pre:
        x = x + _dot(o_ref[...], wo_ref[...])
    h = _rms(x, g_ref[...]).astype(BF16)
    for c in range(d_ff // FFN_CHUNK):
        sl = slice(c * FFN_CHUNK, (c + 1) * FFN_CHUNK)
        gate = _dot(h, wg_ref[:, sl])
        up = _dot(h, wu_ref[:, sl])
        a_scr[:, sl] = (gate * _sigmoid(gate) * up).astype(BF16)
    y = x + _dot(a_scr[...], wd_ref[...])
    if final:
        y = _rms(y, gf_ref[...])
    out_ref[...] = y


def _ffn(x, g, wg, wu, wd, tm, attn=None, final_g=None, window=None):
    m = x.shape[0]
    d_ff = wg.shape[1]
    if window is None:
        grid = (m // tm,)
        m_out = m
        rows_in = lambda width: pl.BlockSpec((tm, width), lambda i: (i, 0))
        rows_out = pl.BlockSpec((tm, D_MODEL), lambda i: (i, 0))
    else:
        n_seq, t_in, t_skip = window
        per_seq = (t_in - t_skip) // tm
        assert per_seq * tm == t_in - t_skip and t_in % 8 == 0 and t_skip % 8 == 0 and tm % 8 == 0
        grid = (n_seq, per_seq)
        m_out = n_seq * per_seq * tm
        first_row = lambda b, i: pl.multiple_of(b * t_in + t_skip + i * tm, 8)
        rows_in = lambda width: pl.BlockSpec((pl.Element(tm), pl.Element(width)), lambda b, i: (first_row(b, i), 0))
        rows_out = pl.BlockSpec((tm, D_MODEL), lambda b, i: (b * per_seq + i, 0))
    args = [x]
    specs = [rows_in(D_MODEL)]
    if attn is not None:
        o, wo = attn
        args += [o, wo]
        specs += [rows_in(o.shape[1]), _const_spec(wo.shape)]
    args += [g, wg, wu, wd]
    specs += [_const_spec((1, D_MODEL)), _const_spec((D_MODEL, d_ff)),
              _const_spec((D_MODEL, d_ff)), _const_spec((d_ff, D_MODEL))]
    if final_g is not None:
        args.append(final_g)
        specs.append(_const_spec((1, D_MODEL)))
    return pl.pallas_call(
        functools.partial(_ffn_kernel, pre=attn is not None, final=final_g is not None, d_ff=d_ff),
        grid=grid,
        in_specs=specs,
        out_specs=rows_out,
        out_shape=jax.ShapeDtypeStruct((m_out, D_MODEL), F32),
        scratch_shapes=[pltpu.VMEM((tm, d_ff), BF16)],
        compiler_params=_params(*(["parallel"] * len(grid))),
        name="ffn",
    )(*args)


def _kv_kernel(*refs, heads, n_real):
    if heads:
        (x_ref, g_ref, wc_ref, wr_ref, gl_ref, cos_ref, sin_ref, wk_ref, wv_ref, e_ref,
         c_ref, kr_ref, k_ref, v_ref) = refs
    else:
        x_ref, g_ref, wc_ref, wr_ref, gl_ref, cos_ref, sin_ref, c_ref, kr_ref = refs

    def real():
        h = _rms(x_ref[...], g_ref[...]).astype(BF16)
        c = _rms(_dot(h, wc_ref[...]), gl_ref[...])
        r = _dot(h, wr_ref[...])
        kr = r[:, :QK_ROPE] * cos_ref[...] + r[:, QK_ROPE:] * sin_ref[...]
        c_ref[...] = c
        kr_ref[...] = kr
        if heads:
            cb = c.astype(BF16)
            k_ref[...] = (_dot(cb, wk_ref[...]) + _dot(kr.astype(BF16), e_ref[...])).astype(BF16)
            v_ref[...] = _dot(cb, wv_ref[...]).astype(BF16)

    if not heads:
        real()
        return

    t = pl.program_id(1)
    pl.when(t < n_real)(real)

    @pl.when(t >= n_real)
    def _():
        k_ref[...] = jnp.zeros_like(k_ref)
        v_ref[...] = jnp.zeros_like(v_ref)


def _kv_side(x, g, wc, wr, gl, cos, sin, tt, heads=None, t_pad=None):
    b, t, _ = x.shape
    n_real = t // tt
    n_steps = n_real if heads is None else pl.cdiv(t_pad, tt)
    clamp = lambda bi, ti: (bi, jnp.minimum(ti, n_real - 1), 0)
    tab = lambda bi, ti: (jnp.minimum(ti, n_real - 1), 0)
    args = [x, g, wc, wr, gl, cos, sin]
    specs = [
        pl.BlockSpec((None, tt, D_MODEL), clamp),
        _const_spec((1, D_MODEL)),
        _const_spec((D_MODEL, KV_LORA)),
        _const_spec((D_MODEL, 2 * QK_ROPE)),
        _const_spec((1, KV_LORA)),
        pl.BlockSpec((tt, QK_ROPE), tab),
        pl.BlockSpec((tt, QK_ROPE), tab),
    ]
    out_specs = [pl.BlockSpec((None, tt, KV_LORA), clamp), pl.BlockSpec((None, tt, QK_ROPE), clamp)]
    out_shape = [jax.ShapeDtypeStruct((b, t, KV_LORA), F32), jax.ShapeDtypeStruct((b, t, QK_ROPE), F32)]
    if heads is not None:
        args += list(heads)
        hv = N_HEADS * V_HEAD
        specs += [_const_spec((KV_LORA, HP)), _const_spec((KV_LORA, hv)), _const_spec((QK_ROPE, HP))]
        full = lambda bi, ti: (bi, ti, 0)
        out_specs += [pl.BlockSpec((None, tt, HP), full), pl.BlockSpec((None, tt, hv), full)]
        out_shape += [jax.ShapeDtypeStruct((b, t_pad, HP), BF16), jax.ShapeDtypeStruct((b, t_pad, hv), BF16)]
    return pl.pallas_call(
        functools.partial(_kv_kernel, heads=heads is not None, n_real=n_real),
        grid=(b, n_steps),
        in_specs=specs,
        out_specs=out_specs,
        out_shape=out_shape,
        compiler_params=_params("parallel", "arbitrary"),
        name="kv_side",
    )(*args)


def _q_kernel(x_ref, g_ref, wdq_ref, gq_ref, wq_ref, wqr_ref, cos_ref, sin_ref, q_ref):
    h = _rms(x_ref[...], g_ref[...]).astype(BF16)
    cq = _rms(_dot(h, wdq_ref[...]), gq_ref[...]).astype(BF16)
    q = _dot(cq, wq_ref[...])
    qr = _dot(cq, wqr_ref[...])
    cos = cos_ref[...]
    sin = sin_ref[...]
    for hd in range(N_HEADS):
        sl = slice(hd * HEAD_PAD, (hd + 1) * HEAD_PAD)
        q_ref[:, sl] = ((q[:, sl] * cos + qr[:, sl] * sin) * Q_SCALE).astype(BF16)


def _q_proj(x, g, wdq, gq, wq, wqr, cos, sin, tt):
    b, t, _ = x.shape
    full = lambda bi, ti: (bi, ti, 0)
    tab = lambda bi, ti: (ti, 0)
    return pl.pallas_call(
        _q_kernel,
        grid=(b, t // tt),
        in_specs=[
            pl.BlockSpec((None, tt, D_MODEL), full),
            _const_spec((1, D_MODEL)),
            _const_spec((D_MODEL, Q_LORA)),
            _const_spec((1, Q_LORA)),
            _const_spec((Q_LORA, HP)),
            _const_spec((Q_LORA, HP)),
            pl.BlockSpec((tt, HEAD_PAD), tab),
            pl.BlockSpec((tt, HEAD_PAD), tab),
        ],
        out_specs=pl.BlockSpec((None, tt, HP), full),
        out_shape=jax.ShapeDtypeStruct((b, t, HP), BF16),
        compiler_params=_params("parallel", "parallel"),
        name="q_proj",
    )(x, g, wdq, gq, wq, wqr, cos, sin)


def _query_tiles(t):
    tiles = [(q0, min(Q_TILE, t - q0)) for q0 in range(0, t, Q_TILE)]
    if len(tiles) > 1 and tiles[-1][1] < Q_TILE // 2:
        (q0, rows), (_, extra) = tiles[-2:]
        tiles[-2:] = [(q0, rows + extra)]
    return tiles


def _attn_kernel(q_ref, k_ref, v_ref, o_ref, s_scr, p_scr, *, t):
    half = KV_CHUNK // 2
    work = [(q0, rows, e) for q0, rows in _query_tiles(t) for e in range(2)]

    def chunks(q0, rows):
        return range(0, pl.cdiv(q0 + rows, KV_CHUNK) * KV_CHUNK, KV_CHUNK)

    def scores(w, q0, rows, e):
        lanes = slice(e * HEAD_PAD, (e + 1) * HEAD_PAD)
        s_buf = s_scr.at[w % 2]
        q = q_ref[q0:q0 + rows, lanes]
        mx = None
        for ks in chunks(q0, rows):
            s = _dot_t(q, k_ref[ks:ks + KV_CHUNK, lanes])
            if ks + KV_CHUNK - 1 > q0:
                qpos = q0 + lax.broadcasted_iota(jnp.int32, s.shape, 0)
                kpos = ks + lax.broadcasted_iota(jnp.int32, s.shape, 1)
                s = jnp.where(kpos <= qpos, s, -jnp.inf)
            s_buf[0:rows, ks:ks + KV_CHUNK] = s
            cm = jnp.maximum(s[:, :half], s[:, half:])
            mx = cm if mx is None else jnp.maximum(mx, cm)
        return jnp.max(mx, axis=1, keepdims=True)

    def probs(w, q0, rows, e, m):
        s_buf = s_scr.at[w % 2]
        p_buf = p_scr.at[w % 2]
        sm = None
        for ks in chunks(q0, rows):
            p = jnp.exp2(s_buf[0:rows, ks:ks + KV_CHUNK] - m)
            p_buf[0:rows, ks:ks + KV_CHUNK] = p.astype(BF16)
            ps = p[:, :half] + p[:, half:]
            sm = ps if sm is None else sm + ps
        return jnp.sum(sm, axis=1, keepdims=True)

    def values(w, q0, rows, e, l):
        n_keys = chunks(q0, rows).stop
        return _dot(p_scr[w % 2, 0:rows, 0:n_keys], v_ref[0:n_keys, :]) / l

    m_of, l_of, held = {}, {}, None
    for step in range(len(work) + 2):
        if step < len(work):
            m_of[step] = scores(step, *work[step])
        if 0 <= step - 1 < len(work):
            l_of[step - 1] = probs(step - 1, *work[step - 1], m_of.pop(step - 1))
        if 0 <= step - 2 < len(work):
            q0, rows, e = work[step - 2]
            o = values(step - 2, q0, rows, e, l_of.pop(step - 2))
            if e == 0:
                held = o
            else:
                lane = lax.broadcasted_iota(jnp.int32, o.shape, 1)
                o_ref[q0:q0 + rows, :] = jnp.where(lane < V_HEAD, held, o).astype(BF16)


def _attn_prompt(q, k, v):
    b, t, _ = q.shape
    t_pad = k.shape[1]
    pair = 2 * HEAD_PAD
    tile_rows = max(rows for _, rows in _query_tiles(t))
    return pl.pallas_call(
        functools.partial(_attn_kernel, t=t),
        grid=(b, N_HEADS // 2),
        in_specs=[pl.BlockSpec((None, t, pair), lambda bi, hi: (bi, 0, hi)),
                  pl.BlockSpec((None, t_pad, pair), lambda bi, hi: (bi, 0, hi)),
                  pl.BlockSpec((None, t_pad, 2 * V_HEAD), lambda bi, hi: (bi, 0, hi))],
        out_specs=pl.BlockSpec((None, t, 2 * V_HEAD), lambda bi, hi: (bi, 0, hi)),
        out_shape=jax.ShapeDtypeStruct((b, t, N_HEADS * V_HEAD), BF16),
        scratch_shapes=[pltpu.VMEM((2, tile_rows, t_pad), F32), pltpu.VMEM((2, tile_rows, t_pad), BF16)],
        compiler_params=_params("parallel", "parallel"),
        name="attn_prompt",
    )(q, k, v)


def _qlat_kernel(q_ref, w_ref, o_ref):
    o_ref[...] = _dot(q_ref[...], w_ref[...]).astype(BF16)


def _q_latent(q, w_uk_t):
    m = q.shape[0]
    return pl.pallas_call(
        _qlat_kernel,
        grid=(N_HEADS,),
        in_specs=[pl.BlockSpec((m, HEAD_PAD), lambda h: (0, h)),
                  pl.BlockSpec((None, HEAD_PAD, KV_LORA), lambda h: (h, 0, 0))],
        out_specs=pl.BlockSpec((None, m, KV_LORA), lambda h: (h, 0, 0)),
        out_shape=jax.ShapeDtypeStruct((N_HEADS, m, KV_LORA), BF16),
        compiler_params=_params("parallel"),
        name="q_latent",
    )(q, w_uk_t)


def _decode_kernel(pt_ref, ql_ref, qr_ref, cn_ref, krn_ref, lat_hbm, krt_hbm, o_ref,
                   lat_buf, kr_buf, lat_sem, kr_sem, m_scr, l_scr, acc_scr, *, groups):
    n_seq = ql_ref.shape[0]
    n_iter = n_seq * groups // 2
    lanes = lambda a, n: jnp.concatenate([a] * n, axis=1)

    def copies(t, slot):
        b = t // groups
        first_page = (t % groups) * DEC_PAGES
        out = []
        for j in range(DEC_PAGES):
            page = pt_ref[b, first_page + j]
            out.append(pltpu.make_async_copy(
                lat_hbm.at[page], lat_buf.at[slot, pl.ds(j * PAGE_SIZE, PAGE_SIZE)], lat_sem.at[slot]))
            out.append(pltpu.make_async_copy(krt_hbm.at[page], kr_buf.at[slot, j], kr_sem.at[slot]))
        return out

    def start(t, slot):
        for cp in copies(t, slot):
            cp.start()

    def wait(t, slot):
        for cp in copies(t, slot):
            cp.wait()

    def group(slot, ql, qr):
        c = lat_buf[slot].astype(BF16)
        kr_t = jnp.concatenate([kr_buf[slot, j] for j in range(DEC_PAGES)], axis=1).astype(BF16)
        s = _dot_t(ql, c) + _dot(qr, kr_t)
        m = jnp.max(s, axis=1, keepdims=True)
        p = jnp.exp2(s - m)
        return m, jnp.sum(p, axis=1, keepdims=True), _dot(p.astype(BF16), c)

    start(0, 0)
    start(1, 1)

    def body(i, carry):
        base = (i % 2) * 2
        b = (2 * i) // groups
        ql = ql_ref[b]
        qr = qr_ref[b]

        @pl.when(i + 1 < n_iter)
        def _():
            start(2 * i + 2, 2 - base)
            start(2 * i + 3, 3 - base)

        @pl.when((2 * i) % groups == 0)
        def _():
            cn = cn_ref[b].astype(BF16).astype(F32)
            krn = krn_ref[b].astype(BF16).astype(F32)
            s_new = (jnp.sum(ql.astype(F32) * cn, axis=1, keepdims=True)
                     + jnp.sum(qr.astype(F32) * krn, axis=1, keepdims=True))
            m_scr[...] = jnp.broadcast_to(s_new, m_scr.shape)
            l_scr[...] = jnp.ones_like(l_scr)
            acc_scr[...] = jnp.broadcast_to(cn, acc_scr.shape)

        wait(2 * i, base)
        wait(2 * i + 1, base + 1)
        m0, l0, o0 = group(base, ql, qr)
        m1, l1, o1 = group(base + 1, ql, qr)

        m_prev = m_scr[...]
        m_new = jnp.maximum(m_prev, jnp.maximum(m0, m1))
        a_prev = jnp.exp2(m_prev - m_new)
        a0 = jnp.exp2(m0 - m_new)
        a1 = jnp.exp2(m1 - m_new)
        l_scr[...] = a_prev * l_scr[...] + a0 * l0 + a1 * l1
        acc_scr[...] = lanes(a_prev, 2) * acc_scr[...] + lanes(a0, 2) * o0 + lanes(a1, 2) * o1
        m_scr[...] = m_new

        @pl.when((2 * i + 2) % groups == 0)
        def _():
            o_ref[b] = acc_scr[...] / lanes(l_scr[...], 2)

        return carry

    lax.fori_loop(0, n_iter, body, 0)


def _decode_attn(page_table, q_lat, q_rope, c_new, kr_new, cache_latent, cache_krope_t):
    b, n_pages = page_table.shape
    groups = n_pages // DEC_PAGES
    assert n_pages % DEC_PAGES == 0 and groups % 2 == 0
    vmem = pl.BlockSpec(memory_space=pltpu.VMEM)
    return pl.pallas_call(
        functools.partial(_decode_kernel, groups=groups),
        in_specs=[pl.BlockSpec(memory_space=pltpu.SMEM), vmem, vmem, vmem, vmem,
                  pl.BlockSpec(memory_space=pl.ANY), pl.BlockSpec(memory_space=pl.ANY)],
        out_specs=vmem,
        out_shape=jax.ShapeDtypeStruct((b, N_HEADS, KV_LORA), F32),
        scratch_shapes=[
            pltpu.VMEM((DEC_SLOTS, DEC_PAGES * PAGE_SIZE, KV_LORA), F32),
            pltpu.VMEM((DEC_SLOTS, DEC_PAGES, QK_ROPE, PAGE_SIZE), F32),
            pltpu.SemaphoreType.DMA((DEC_SLOTS,)),
            pltpu.SemaphoreType.DMA((DEC_SLOTS,)),
            pltpu.VMEM((N_HEADS, HEAD_PAD), F32),
            pltpu.VMEM((N_HEADS, HEAD_PAD), F32),
            pltpu.VMEM((N_HEADS, KV_LORA), F32),
        ],
        compiler_params=pltpu.CompilerParams(vmem_limit_bytes=VMEM_LIMIT),
        name="decode_attn",
    )(page_table, q_lat, q_rope, c_new, kr_new, cache_latent, cache_krope_t)


def _olat_kernel(o_ref, w_ref, out_ref):
    out_ref[...] = (_dot(o_ref[0].astype(BF16), w_ref[0]) + _dot(o_ref[1].astype(BF16), w_ref[1])).astype(BF16)


def _o_latent_proj(o_lat, w_uv_pairs):
    m = o_lat.shape[1]
    return pl.pallas_call(
        _olat_kernel,
        grid=(N_HEADS // 2,),
        in_specs=[pl.BlockSpec((2, m, KV_LORA), lambda h: (h, 0, 0)),
                  pl.BlockSpec((2, KV_LORA, LANES), lambda h: (h, 0, 0))],
        out_specs=pl.BlockSpec((m, LANES), lambda h: (0, h)),
        out_shape=jax.ShapeDtypeStruct((m, N_HEADS * V_HEAD), BF16),
        compiler_params=_params("parallel"),
        name="o_latent_proj",
    )(o_lat, w_uv_pairs)


def _rope_tables(pos):
    half = QK_ROPE // 2
    inv = jnp.power(ROPE_BASE, -jnp.arange(half, dtype=F32) / half)
    ang = pos.astype(F32)[:, None] * inv[None, :]
    cos = jnp.cos(ang)
    sin = jnp.sin(ang)
    cos32 = jnp.concatenate([cos, cos], axis=1)
    sin32 = jnp.concatenate([sin, sin], axis=1)
    n = pos.shape[0]
    cos128 = jnp.concatenate([jnp.ones((n, QK_NOPE), F32), cos32, jnp.zeros((n, HEAD_PAD - QK_NOPE - QK_ROPE), F32)], axis=1)
    sin128 = jnp.concatenate([jnp.zeros((n, QK_NOPE), F32), sin32, jnp.zeros((n, HEAD_PAD - QK_NOPE - QK_ROPE), F32)], axis=1)
    return cos32, sin32, cos128, sin128


def _rot_half_cols(w):
    half = QK_ROPE // 2
    return jnp.concatenate([-w[..., half:], w[..., :half]], axis=-1)


def _pad_heads(w, width):
    r = w.shape[0]
    return jnp.pad(w, ((0, 0), (0, 0), (0, HEAD_PAD - width))).reshape(r, HP)


def kernel(x_prompt, x_sample, cache_latent, cache_krope, state_conv, page_table, meta_tokens, a_norm_g, a_pw1_w, a_pw1_b, a_dw_w, a_dw_b, a_ln_g, a_ln_b, a_pw2_w, a_pw2_b, ffn_norm_g, ffn_w_gate, ffn_w_up, ffn_w_down, kv_norm_g, w_dkv, kv_latent_norm_g, w_uk, w_uv, b_norm_g, b_w_dq, b_q_norm_g, b_w_uq, b_w_o, final_norm_g):
    n_a = a_norm_g.shape[0]
    depth = ffn_norm_g.shape[0]
    n_b = depth - n_a
    bp, seq, _ = x_prompt.shape
    t_p = seq + N_META
    bs = x_sample.shape[0]
    past_len = page_table.shape[1] * PAGE_SIZE

    row = lambda v: v.reshape(1, -1).astype(F32)

    pw1_w = a_pw1_w.astype(BF16)
    pw2_w = a_pw2_w.astype(BF16)
    w_gate = ffn_w_gate.astype(BF16)
    w_up = ffn_w_up.astype(BF16)
    w_down = ffn_w_down.astype(BF16)
    first_tap = HIST - (CONV_W - 1)
    dw_tiles = jnp.pad(a_dw_w, ((0, 0), (first_tap, HIST + 8 - first_tap - CONV_W), (0, 0)))
    dw_tiles = dw_tiles.reshape(n_a, HIST + 8, 8, LANES)
    w_c = w_dkv[:, :KV_LORA].astype(BF16)
    w_kr = w_dkv[:, KV_LORA:]
    w_r = jnp.concatenate([w_kr, _rot_half_cols(w_kr)], axis=1).astype(BF16)
    w_k_pad = _pad_heads(w_uk, QK_NOPE).astype(BF16)
    w_v = w_uv.reshape(KV_LORA, N_HEADS * V_HEAD).astype(BF16)
    place = jnp.zeros((QK_ROPE, N_HEADS, HEAD_PAD), F32)
    place = place.at[jnp.arange(QK_ROPE), :, QK_NOPE + jnp.arange(QK_ROPE)].set(1.0)
    place = place.reshape(QK_ROPE, HP).astype(BF16)
    w_dq = b_w_dq.astype(BF16)
    w_q_pad = jnp.stack([_pad_heads(b_w_uq[j], QK_NOPE + QK_ROPE) for j in range(n_b)]).astype(BF16)
    w_q_rot = jnp.stack([
        _pad_heads(jnp.concatenate([jnp.zeros((Q_LORA, N_HEADS, QK_NOPE), F32),
                                    _rot_half_cols(b_w_uq[j][..., QK_NOPE:])], axis=-1), QK_NOPE + QK_ROPE)
        for j in range(n_b)]).astype(BF16)
    w_o = b_w_o.astype(BF16)
    w_uk_t = jnp.pad(jnp.transpose(w_uk, (1, 2, 0)), ((0, 0), (0, HEAD_PAD - QK_NOPE), (0, 0))).astype(BF16)
    w_uv_h = jnp.transpose(w_uv, (1, 0, 2))
    w_uv_pairs = jnp.where((jnp.arange(N_HEADS) % 2 == 0)[:, None, None],
                           jnp.pad(w_uv_h, ((0, 0), (0, 0), (0, V_HEAD))),
                           jnp.pad(w_uv_h, ((0, 0), (0, 0), (V_HEAD, 0)))).astype(BF16)

    def ffn(x, l, tm, attn=None, window=None):
        final = row(final_norm_g) if l == depth - 1 else None
        return _ffn(x, row(ffn_norm_g[l]), w_gate[l], w_up[l], w_down[l], tm, attn=attn, final_g=final, window=window)

    def conv_args(l):
        return (row(a_ln_g[l]), row(a_ln_b[l]), pw2_w[l], row(a_pw2_b[l]))

    def kv_args():
        return (row(kv_norm_g), w_c, w_r, row(kv_latent_norm_g))

    def q_args(j):
        return (row(b_norm_g[j]), w_dq[j], row(b_q_norm_g[j]), w_q_pad[j], w_q_rot[j])

    xp = jnp.concatenate([jnp.broadcast_to(meta_tokens[None].astype(F32), (bp, N_META, D_MODEL)), x_prompt], axis=1)
    cos32, sin32, cos128, sin128 = _rope_tables(jnp.arange(t_p))
    m_p = bp * t_p
    t_pad = pl.cdiv(t_p, KV_CHUNK) * KV_CHUNK
    x = xp.reshape(m_p, D_MODEL)
    conv_p = []
    for l in range(n_a):
        glu = _pw1_glu(x, row(a_norm_g[l]), pw1_w[l], row(a_pw1_b[l]), ROW_TILE, tiles=True)
        glu = glu.reshape(bp, t_p, 8, LANES)
        conv_p.append(glu[:, t_p - (CONV_W - 1):].reshape(bp, CONV_W - 1, D_MODEL))
        x = _conv_block(glu, x.reshape(bp, t_p, D_MODEL), dw_tiles[l], a_dw_b[l].reshape(8, LANES),
                        *conv_args(l), SEQ_TILE).reshape(m_p, D_MODEL)
        x = ffn(x, l, ROW_TILE)
    c_p, kr_p, k_p, v_p = _kv_side(x.reshape(bp, t_p, D_MODEL), *kv_args(), cos32, sin32, SEQ_TILE,
                                   heads=(w_k_pad, w_v, place), t_pad=t_pad)
    for j in range(n_b):
        q = _q_proj(x.reshape(bp, t_p, D_MODEL), *q_args(j), cos128, sin128, SEQ_TILE)
        o = _attn_prompt(q, k_p, v_p).reshape(m_p, N_HEADS * V_HEAD)
        if n_a + j < depth - 1:
            x = ffn(x, n_a + j, ROW_TILE, attn=(o, w_o[j]))
        else:
            x = ffn(x, n_a + j, OUT_TILE, attn=(o, w_o[j]), window=(bp, t_p, N_META))
    y_prompt = x.reshape(bp, seq, D_MODEL)

    pos_s = jnp.full((bs,), past_len, jnp.int32)
    cos32s, sin32s, cos128s, sin128s = _rope_tables(pos_s)
    x = x_sample.reshape(bs, D_MODEL)
    conv_s = []
    state_t = jnp.transpose(state_conv, (0, 2, 1, 3))
    krope_t = jnp.swapaxes(cache_krope, 1, 2)
    for l in range(n_a):
        glu = _pw1_glu(x, row(a_norm_g[l]), pw1_w[l], row(a_pw1_b[l]), bs, tiles=False)
        conv_s.append(jnp.concatenate([state_conv[l][:, 1:], glu[:, None, :]], axis=1))
        x = _conv_step(state_t[l], glu, x, a_dw_w[l], row(a_dw_b[l]), *conv_args(l))
        x = ffn(x, l, bs)
    c_s, kr_s = _kv_side(x[None], *kv_args(), cos32s, sin32s, bs)
    c_s = c_s.reshape(bs, 1, KV_LORA)
    kr_s = kr_s.reshape(bs, 1, QK_ROPE)
    for j in range(n_b):
        q = _q_proj(x[None], *q_args(j), cos128s, sin128s, bs)[0]
        q_lat = jnp.transpose(_q_latent(q, w_uk_t), (1, 0, 2))
        q_rope = q.reshape(bs, N_HEADS, HEAD_PAD)[:, :, QK_NOPE:QK_NOPE + QK_ROPE]
        o_lat = _decode_attn(page_table, q_lat, q_rope, c_s, kr_s, cache_latent, krope_t)
        o = _o_latent_proj(jnp.transpose(o_lat, (1, 0, 2)), w_uv_pairs)
        x = ffn(x, n_a + j, bs, attn=(o, w_o[j]))
    y_sample = x.reshape(bs, 1, D_MODEL)

    return (y_prompt, y_sample, c_p, kr_p, jnp.stack(conv_p), c_s, kr_s, jnp.stack(conv_s))
```

```python
import functools
import math

import jax
import jax.numpy as jnp
from jax import lax
from jax.experimental import pallas as pl
from jax.experimental.pallas import tpu as pltpu

D_MODEL = 1024
N_HEADS = 16
QK_NOPE = 64
QK_ROPE = 32
V_HEAD = 64
KV_LORA = 256
Q_LORA = 384
CONV_W = 31
N_META = 16
PAGE_SIZE = 128
ROPE_BASE = 10000.0
EPS = 1e-6
SCALE = 1.0 / math.sqrt(QK_NOPE + QK_ROPE)
Q_SCALE = SCALE * math.log2(math.e)

LANES = 128
HEAD_PAD = LANES
HP = N_HEADS * HEAD_PAD

BF16 = jnp.bfloat16
F32 = jnp.float32

VMEM_LIMIT = 56 * 1024 * 1024

ROW_TILE = 384
SEQ_TILE = 688
OUT_TILE = 512
Q_TILE = 256
KV_CHUNK = 256
FFN_CHUNK = 256
PW1_TILE = 1376
PW1_PARTS = 2
PW1_CHUNK = 256
CONV_STEPS = 16
HIST = 32
DEC_PAGES = 32
DEC_SLOTS = 4


def _params(*sem):
    return pltpu.CompilerParams(dimension_semantics=sem, vmem_limit_bytes=VMEM_LIMIT)


def _const_spec(shape, layer=None):
    n = len(shape)
    if layer is None:
        return pl.BlockSpec(shape, lambda *_: (0,) * n, pipeline_mode=pl.Buffered(1))
    return pl.BlockSpec((None, *shape), lambda *_: (layer,) + (0,) * n, pipeline_mode=pl.Buffered(1))


def _rms(x, g):
    return x * lax.rsqrt(jnp.mean(x * x, axis=-1, keepdims=True) + EPS) * g


def _sigmoid(x):
    return 1.0 / (1.0 + jnp.exp(-x))


def _dot(a, b):
    return jnp.dot(a, b, preferred_element_type=F32)


def _dot_t(a, b):
    return lax.dot_general(a, b, (((1,), (1,)), ((), ())), preferred_element_type=F32)


def _swap8(cols):
    cols = list(cols)
    sub = lax.broadcasted_iota(jnp.int32, cols[0].shape, 1)
    for d in (4, 2, 1):
        keep = (sub & d) == 0
        for j in range(8):
            if j & d:
                continue
            a, b = cols[j], cols[j + d]
            cols[j] = jnp.where(keep, a, pltpu.roll(b, d, axis=1))
            cols[j + d] = jnp.where(keep, pltpu.roll(a, 8 - d, axis=1), b)
    return cols


def _pw1_kernel(x_ref, g_ref, w_ref, b_ref, o_ref, *, tiles, parts):
    rows = x_ref.shape[0] // parts
    for part in range(parts):
        r0 = part * rows
        h = _rms(x_ref[r0:r0 + rows, :], g_ref[...]).astype(BF16)
        glu = []
        for c0 in range(0, D_MODEL, PW1_CHUNK):
            val = _dot(h, w_ref[:, c0:c0 + PW1_CHUNK]) + b_ref[:, c0:c0 + PW1_CHUNK]
            gate = (_dot(h, w_ref[:, D_MODEL + c0:D_MODEL + c0 + PW1_CHUNK])
                    + b_ref[:, D_MODEL + c0:D_MODEL + c0 + PW1_CHUNK])
            glu.append(val * _sigmoid(gate))
        if not tiles:
            o_ref[r0:r0 + rows, :] = jnp.concatenate(glu, axis=1)
            continue
        per = PW1_CHUNK // LANES
        cols = _swap8([glu[j // per][:, LANES * (j % per):LANES * (j % per + 1)].reshape(rows // 8, 8, LANES)
                       for j in range(8)])
        for t in range(8):
            o_ref[r0 // 8:(r0 + rows) // 8, t] = cols[t]


def _pw1_glu(x, g, w, b, tm, tiles, layer):
    m = x.shape[0]
    if tiles:
        out_spec = pl.BlockSpec((tm // 8, 8, 8, LANES), lambda i: (i, 0, 0, 0))
        out_shape = jax.ShapeDtypeStruct((m // 8, 8, 8, LANES), F32)
    else:
        out_spec = pl.BlockSpec((tm, D_MODEL), lambda i: (i, 0))
        out_shape = jax.ShapeDtypeStruct((m, D_MODEL), F32)
    return pl.pallas_call(
        functools.partial(_pw1_kernel, tiles=tiles, parts=PW1_PARTS if tm % (8 * PW1_PARTS) == 0 and tiles else 1),
        grid=(m // tm,),
        in_specs=[
            pl.BlockSpec((tm, D_MODEL), lambda i: (i, 0)),
            _const_spec((1, D_MODEL)),
            _const_spec((D_MODEL, 2 * D_MODEL), layer),
            _const_spec((1, 2 * D_MODEL)),
        ],
        out_specs=out_spec,
        out_shape=out_shape,
        compiler_params=_params("parallel"),
        name="pw1_glu",
    )(x, g, w, b)


def _ln_swish_pw2(y, x, lng_ref, lnb_ref, w2_ref, b2_ref):
    mu = jnp.mean(y, axis=-1, keepdims=True)
    yc = y - mu
    yn = yc * lax.rsqrt(jnp.mean(yc * yc, axis=-1, keepdims=True) + EPS)
    yn = yn * lng_ref[...] + lnb_ref[...]
    z = (yn * _sigmoid(yn)).astype(BF16)
    return x + _dot(z, w2_ref[...]) + b2_ref[...]


def _conv_kernel(g_ref, x_ref, dw_ref, dwb_ref, lng_ref, lnb_ref, w2_ref, b2_ref, o_ref,
                 buf, y_scr, *, tt):
    t = pl.program_id(1)

    @pl.when(t == 0)
    def _():
        buf[0:HIST] = jnp.zeros((HIST, 8, LANES), F32)

    @pl.when(t > 0)
    def _():
        buf[0:HIST] = buf[tt:tt + HIST]

    buf[HIST:HIST + tt] = g_ref[...]
    bias = jnp.broadcast_to(dwb_ref[...], (CONV_STEPS, 8, LANES))

    def block(i, carry):
        rs = i * CONV_STEPS
        acc = bias
        for o in range(HIST - (CONV_W - 1), HIST + 1):
            acc = acc + buf[pl.ds(rs + o, CONV_STEPS)] * dw_ref[o]
        y_scr[pl.ds(i * (CONV_STEPS // 8), CONV_STEPS // 8)] = acc.reshape(CONV_STEPS // 8, 8, 8, LANES)
        return carry

    lax.fori_loop(0, tt // CONV_STEPS, block, 0)
    cols = _swap8([y_scr[:, s] for s in range(8)])
    y = jnp.concatenate([c.reshape(tt, LANES) for c in cols], axis=1)
    o_ref[...] = _ln_swish_pw2(y, x_ref[...], lng_ref, lnb_ref, w2_ref, b2_ref)


def _conv_block(glu, x, dw_tiles, dwb_tile, lng, lnb, w2, b2, tt, layer):
    b, t, _ = x.shape
    assert tt % CONV_STEPS == 0 and t % tt == 0
    rows = pl.BlockSpec((None, tt, D_MODEL), lambda bi, ti: (bi, ti, 0))
    return pl.pallas_call(
        functools.partial(_conv_kernel, tt=tt),
        grid=(b, t // tt),
        in_specs=[
            pl.BlockSpec((None, tt, 8, LANES), lambda bi, ti: (bi, ti, 0, 0)),
            rows,
            _const_spec((HIST + 8, 8, LANES)),
            _const_spec((8, LANES)),
            _const_spec((1, D_MODEL)),
            _const_spec((1, D_MODEL)),
            _const_spec((D_MODEL, D_MODEL), layer),
            _const_spec((1, D_MODEL)),
        ],
        out_specs=rows,
        out_shape=jax.ShapeDtypeStruct((b, t, D_MODEL), F32),
        scratch_shapes=[
            pltpu.VMEM((HIST + tt, 8, LANES), F32),
            pltpu.VMEM((tt // 8, 8, 8, LANES), F32),
        ],
        compiler_params=_params("parallel", "arbitrary"),
        name="conv_block",
    )(glu, x, dw_tiles, dwb_tile, lng, lnb, w2, b2)


def _conv_step_kernel(st_ref, g_ref, x_ref, dw_ref, dwb_ref, lng_ref, lnb_ref, w2_ref, b2_ref, o_ref):
    g = g_ref[...]
    y = g * dw_ref[CONV_W - 1:CONV_W, :] + dwb_ref[...]
    for k in range(CONV_W - 1):
        y = y + st_ref[k] * dw_ref[k:k + 1, :]
    o_ref[...] = _ln_swish_pw2(y, x_ref[...], lng_ref, lnb_ref, w2_ref, b2_ref)


def _conv_step(state, glu, x, dw, dwb, lng, lnb, w2, b2):
    m = x.shape[0]
    return pl.pallas_call(
        _conv_step_kernel,
        out_shape=jax.ShapeDtypeStruct((m, D_MODEL), F32),
        compiler_params=pltpu.CompilerParams(vmem_limit_bytes=VMEM_LIMIT),
        name="conv_step",
    )(state, glu, x, dw, dwb, lng, lnb, w2, b2)


def _ffn_kernel(*refs, pre, final, d_ff):
    it = iter(refs)
    x_ref = next(it)
    if pre:
        o_ref, wo_ref = next(it), next(it)
    g_ref, wg_ref, wu_ref, wd_ref = next(it), next(it), next(it), next(it)
    if final:
        gf_ref = next(it)
    out_ref, a_scr = next(it), next(it)

    x = x_ref[...]
    if pre:
        x = x + _dot(o_ref[...], wo_ref[...])
    h = _rms(x, g_ref[...]).astype(BF16)
    for c in range(d_ff // FFN_CHUNK):
        sl = slice(c * FFN_CHUNK, (c + 1) * FFN_CHUNK)
        gate = _dot(h, wg_ref[:, sl])
        up = _dot(h, wu_ref[:, sl])
        a_scr[:, sl] = (gate * _sigmoid(gate) * up).astype(BF16)
    y = x + _dot(a_scr[...], wd_ref[...])
    if final:
        y = _rms(y, gf_ref[...])
    out_ref[...] = y


def _ffn(x, g, wg, wu, wd, layer, tm, attn=None, final_g=None, window=None):
    m = x.shape[0]
    d_ff = wg.shape[2]
    if window is None:
        grid = (m // tm,)
        m_out = m
        rows_in = lambda width: pl.BlockSpec((tm, width), lambda i: (i, 0))
        rows_out = pl.BlockSpec((tm, D_MODEL), lambda i: (i, 0))
    else:
        n_seq, t_in, t_skip = window
        per_seq = (t_in - t_skip) // tm
        assert per_seq * tm == t_in - t_skip and t_in % 8 == 0 and t_skip % 8 == 0 and tm % 8 == 0
        grid = (n_seq, per_seq)
        m_out = n_seq * per_seq * tm
        first_row = lambda b, i: pl.multiple_of(b * t_in + t_skip + i * tm, 8)
        rows_in = lambda width: pl.BlockSpec((pl.Element(tm), pl.Element(width)), lambda b, i: (first_row(b, i), 0))
        rows_out = pl.BlockSpec((tm, D_MODEL), lambda b, i: (b * per_seq + i, 0))
    args = [x]
    specs = [rows_in(D_MODEL)]
    if attn is not None:
        o, wo, j = attn
        args += [o, wo]
        specs += [rows_in(o.shape[1]), _const_spec(wo.shape[1:], j)]
    args += [g, wg, wu, wd]
    specs += [_const_spec((1, D_MODEL)), _const_spec((D_MODEL, d_ff), layer),
              _const_spec((D_MODEL, d_ff), layer), _const_spec((d_ff, D_MODEL), layer)]
    if final_g is not None:
        args.append(final_g)
        specs.append(_const_spec((1, D_MODEL)))
    return pl.pallas_call(
        functools.partial(_ffn_kernel, pre=attn is not None, final=final_g is not None, d_ff=d_ff),
        grid=grid,
        in_specs=specs,
        out_specs=rows_out,
        out_shape=jax.ShapeDtypeStruct((m_out, D_MODEL), F32),
        scratch_shapes=[pltpu.VMEM((tm, d_ff), BF16)],
        compiler_params=_params(*(["parallel"] * len(grid))),
        name="ffn",
    )(*args)


def _kv_kernel(*refs, heads, n_real):
    if heads:
        (x_ref, g_ref, w_ref, gl_ref, cos_ref, sin_ref, wk_ref, wv_ref, e_ref,
         c_ref, kr_ref, k_ref, v_ref) = refs
    else:
        x_ref, g_ref, w_ref, gl_ref, cos_ref, sin_ref, c_ref, kr_ref = refs

    def real():
        h = _rms(x_ref[...], g_ref[...]).astype(BF16)
        ckr = _dot(h, w_ref[...])
        c = _rms(ckr[:, :KV_LORA], gl_ref[...])
        kr = (ckr[:, KV_LORA:KV_LORA + QK_ROPE] * cos_ref[...]
              + ckr[:, KV_LORA + QK_ROPE:] * sin_ref[...])
        c_ref[...] = c
        kr_ref[...] = kr
        if heads:
            cb = c.astype(BF16)
            k_nope = _dot(cb, wk_ref[...])
            k_rope = _dot(kr.astype(BF16), e_ref[...])
            for hd in range(N_HEADS):
                sl = slice(hd * HEAD_PAD, (hd + 1) * HEAD_PAD)
                k_ref[:, sl] = (k_nope[:, sl] + k_rope).astype(BF16)
            v_ref[...] = _dot(cb, wv_ref[...]).astype(BF16)

    if not heads:
        real()
        return

    t = pl.program_id(1)
    pl.when(t < n_real)(real)

    @pl.when(t >= n_real)
    def _():
        k_ref[...] = jnp.zeros_like(k_ref)
        v_ref[...] = jnp.zeros_like(v_ref)


def _kv_side(x, g, w, gl, cos, sin, tt, heads=None, t_pad=None):
    b, t, _ = x.shape
    n_real = t // tt
    n_steps = n_real if heads is None else pl.cdiv(t_pad, tt)
    clamp = lambda bi, ti: (bi, jnp.minimum(ti, n_real - 1), 0)
    tab = lambda bi, ti: (jnp.minimum(ti, n_real - 1), 0)
    args = [x, g, w, gl, cos, sin]
    specs = [
        pl.BlockSpec((None, tt, D_MODEL), clamp),
        _const_spec((1, D_MODEL)),
        _const_spec((D_MODEL, KV_LORA + 2 * QK_ROPE)),
        _const_spec((1, KV_LORA)),
        pl.BlockSpec((tt, QK_ROPE), tab),
        pl.BlockSpec((tt, QK_ROPE), tab),
    ]
    out_specs = [pl.BlockSpec((None, tt, KV_LORA), clamp), pl.BlockSpec((None, tt, QK_ROPE), clamp)]
    out_shape = [jax.ShapeDtypeStruct((b, t, KV_LORA), F32), jax.ShapeDtypeStruct((b, t, QK_ROPE), F32)]
    if heads is not None:
        args += list(heads)
        hv = N_HEADS * V_HEAD
        specs += [_const_spec((KV_LORA, HP)), _const_spec((KV_LORA, hv)), _const_spec((QK_ROPE, HEAD_PAD))]
        full = lambda bi, ti: (bi, ti, 0)
        out_specs += [pl.BlockSpec((None, tt, HP), full), pl.BlockSpec((None, tt, hv), full)]
        out_shape += [jax.ShapeDtypeStruct((b, t_pad, HP), BF16), jax.ShapeDtypeStruct((b, t_pad, hv), BF16)]
    return pl.pallas_call(
        functools.partial(_kv_kernel, heads=heads is not None, n_real=n_real),
        grid=(b, n_steps),
        in_specs=specs,
        out_specs=out_specs,
        out_shape=out_shape,
        compiler_params=_params("parallel", "arbitrary"),
        name="kv_side",
    )(*args)


def _q_kernel(x_ref, g_ref, wdq_ref, gq_ref, wq_ref, tab_ref, q_ref):
    h = _rms(x_ref[...], g_ref[...]).astype(BF16)
    cq = _rms(_dot(h, wdq_ref[...]), gq_ref[...]).astype(BF16)
    q = _dot(cq, wq_ref[...])
    tab = tab_ref[...]
    for hd in range(N_HEADS):
        sl = slice(hd * HEAD_PAD, (hd + 1) * HEAD_PAD)
        q_ref[:, sl] = (q[:, sl] * tab).astype(BF16)


def _q_proj(x, g, wdq, gq, wq, layer, tab, tt):
    b, t, _ = x.shape
    full = lambda bi, ti: (bi, ti, 0)
    return pl.pallas_call(
        _q_kernel,
        grid=(b, t // tt),
        in_specs=[
            pl.BlockSpec((None, tt, D_MODEL), full),
            _const_spec((1, D_MODEL)),
            _const_spec((D_MODEL, Q_LORA), layer),
            _const_spec((1, Q_LORA)),
            _const_spec((Q_LORA, HP), layer),
            pl.BlockSpec((tt, HEAD_PAD), lambda bi, ti: (ti, 0)),
        ],
        out_specs=pl.BlockSpec((None, tt, HP), full),
        out_shape=jax.ShapeDtypeStruct((b, t, HP), BF16),
        compiler_params=_params("parallel", "parallel"),
        name="q_proj",
    )(x, g, wdq, gq, wq, tab)


def _query_tiles(t):
    tiles = [(q0, min(Q_TILE, t - q0)) for q0 in range(0, t, Q_TILE)]
    if len(tiles) > 1 and tiles[-1][1] < Q_TILE // 2:
        (q0, rows), (_, extra) = tiles[-2:]
        tiles[-2:] = [(q0, rows + extra)]
    return tiles


def _attn_kernel(q_ref, k_ref, v_ref, o_ref, s_scr, p_scr, *, t):
    half = KV_CHUNK // 2
    work = [(q0, rows, e) for q0, rows in _query_tiles(t) for e in range(2)]

    def chunks(q0, rows):
        return range(0, pl.cdiv(q0 + rows, KV_CHUNK) * KV_CHUNK, KV_CHUNK)

    def scores(w, q0, rows, e):
        lanes = slice(e * HEAD_PAD, (e + 1) * HEAD_PAD)
        s_buf = s_scr.at[w % 2]
        q = q_ref[q0:q0 + rows, lanes]
        mx = None
        for ks in chunks(q0, rows):
            s = _dot_t(q, k_ref[ks:ks + KV_CHUNK, lanes])
            if ks + KV_CHUNK - 1 > q0:
                qpos = q0 + lax.broadcasted_iota(jnp.int32, s.shape, 0)
                kpos = ks + lax.broadcasted_iota(jnp.int32, s.shape, 1)
                s = jnp.where(kpos <= qpos, s, -jnp.inf)
            s_buf[0:rows, ks:ks + KV_CHUNK] = s
            cm = jnp.maximum(s[:, :half], s[:, half:])
            mx = cm if mx is None else jnp.maximum(mx, cm)
        return jnp.max(mx, axis=1, keepdims=True)

    def probs(w, q0, rows, e, m):
        s_buf = s_scr.at[w % 2]
        p_buf = p_scr.at[w % 2]
        sm = None
        for ks in chunks(q0, rows):
            p = jnp.exp2(s_buf[0:rows, ks:ks + KV_CHUNK] - m)
            p_buf[0:rows, ks:ks + KV_CHUNK] = p.astype(BF16)
            ps = p[:, :half] + p[:, half:]
            sm = ps if sm is None else sm + ps
        return jnp.sum(sm, axis=1, keepdims=True)

    def values(w, q0, rows, e, l):
        n_keys = chunks(q0, rows).stop
        return _dot(p_scr[w % 2, 0:rows, 0:n_keys], v_ref[0:n_keys, :]) / l

    m_of, l_of, held = {}, {}, None
    for step in range(len(work) + 2):
        if step < len(work):
            m_of[step] = scores(step, *work[step])
        if 0 <= step - 1 < len(work):
            l_of[step - 1] = probs(step - 1, *work[step - 1], m_of.pop(step - 1))
        if 0 <= step - 2 < len(work):
            q0, rows, e = work[step - 2]
            o = values(step - 2, q0, rows, e, l_of.pop(step - 2))
            if e == 0:
                held = o
            else:
                lane = lax.broadcasted_iota(jnp.int32, o.shape, 1)
                o_ref[q0:q0 + rows, :] = jnp.where(lane < V_HEAD, held, o).astype(BF16)


def _attn_prompt(q, k, v):
    b, t, _ = q.shape
    t_pad = k.shape[1]
    pair = 2 * HEAD_PAD
    tile_rows = max(rows for _, rows in _query_tiles(t))
    return pl.pallas_call(
        functools.partial(_attn_kernel, t=t),
        grid=(b, N_HEADS // 2),
        in_specs=[pl.BlockSpec((None, t, pair), lambda bi, hi: (bi, 0, hi)),
                  pl.BlockSpec((None, t_pad, pair), lambda bi, hi: (bi, 0, hi)),
                  pl.BlockSpec((None, t_pad, 2 * V_HEAD), lambda bi, hi: (bi, 0, hi))],
        out_specs=pl.BlockSpec((None, t, 2 * V_HEAD), lambda bi, hi: (bi, 0, hi)),
        out_shape=jax.ShapeDtypeStruct((b, t, N_HEADS * V_HEAD), BF16),
        scratch_shapes=[pltpu.VMEM((2, tile_rows, t_pad), F32), pltpu.VMEM((2, tile_rows, t_pad), BF16)],
        compiler_params=_params("parallel", "parallel"),
        name="attn_prompt",
    )(q, k, v)


def _qlat_kernel(q_ref, w_ref, o_ref, r_ref):
    q = q_ref[...]
    o_ref[...] = _dot(q, w_ref[...]).astype(BF16)
    rot = q[:, QK_NOPE:].astype(F32)
    r_ref[...] = (rot[:, :QK_ROPE] + rot[:, QK_ROPE:]).astype(BF16)


def _q_latent(q, w_uk_t):
    m = q.shape[0]
    return pl.pallas_call(
        _qlat_kernel,
        grid=(N_HEADS,),
        in_specs=[pl.BlockSpec((m, HEAD_PAD), lambda h: (0, h)),
                  pl.BlockSpec((None, HEAD_PAD, KV_LORA), lambda h: (h, 0, 0))],
        out_specs=[pl.BlockSpec((None, m, KV_LORA), lambda h: (h, 0, 0)),
                   pl.BlockSpec((None, m, QK_ROPE), lambda h: (h, 0, 0))],
        out_shape=[jax.ShapeDtypeStruct((N_HEADS, m, KV_LORA), BF16),
                   jax.ShapeDtypeStruct((N_HEADS, m, QK_ROPE), BF16)],
        compiler_params=_params("parallel"),
        name="q_latent",
    )(q, w_uk_t)


def _decode_kernel(pt_ref, ql_ref, qr_ref, cn_ref, krn_ref, lat_hbm, krt_hbm, o_ref,
                   lat_buf, kr_buf, lat_sem, kr_sem, m_scr, l_scr, acc_scr, *, groups):
    n_seq = ql_ref.shape[0]
    n_iter = n_seq * groups // 2
    lanes = lambda a, n: jnp.concatenate([a] * n, axis=1)

    def copies(t, slot):
        b = t // groups
        first_page = (t % groups) * DEC_PAGES
        out = []
        for j in range(DEC_PAGES):
            page = pt_ref[b, first_page + j]
            out.append(pltpu.make_async_copy(
                lat_hbm.at[page], lat_buf.at[slot, pl.ds(j * PAGE_SIZE, PAGE_SIZE)], lat_sem.at[slot]))
            out.append(pltpu.make_async_copy(krt_hbm.at[page], kr_buf.at[slot, j], kr_sem.at[slot]))
        return out

    def start(t, slot):
        for cp in copies(t, slot):
            cp.start()

    def wait(t, slot):
        for cp in copies(t, slot):
            cp.wait()

    def group(slot, ql, qr):
        c = lat_buf[slot].astype(BF16)
        kr_t = jnp.concatenate([kr_buf[slot, j] for j in range(DEC_PAGES)], axis=1).astype(BF16)
        s = _dot_t(ql, c) + _dot(qr, kr_t)
        m = jnp.max(s, axis=1, keepdims=True)
        p = jnp.exp2(s - m)
        return m, jnp.sum(p, axis=1, keepdims=True), _dot(p.astype(BF16), c)

    start(0, 0)
    start(1, 1)

    def body(i, carry):
        base = (i % 2) * 2
        b = (2 * i) // groups
        ql = ql_ref[b]
        qr = qr_ref[b]

        @pl.when(i + 1 < n_iter)
        def _():
            start(2 * i + 2, 2 - base)
            start(2 * i + 3, 3 - base)

        @pl.when((2 * i) % groups == 0)
        def _():
            cn = cn_ref[b].astype(BF16).astype(F32)
            krn = krn_ref[b].astype(BF16).astype(F32)
            s_new = (jnp.sum(ql.astype(F32) * cn, axis=1, keepdims=True)
                     + jnp.sum(qr.astype(F32) * krn, axis=1, keepdims=True))
            m_scr[...] = jnp.broadcast_to(s_new, m_scr.shape)
            l_scr[...] = jnp.ones_like(l_scr)
            acc_scr[...] = jnp.broadcast_to(cn, acc_scr.shape)

        wait(2 * i, base)
        wait(2 * i + 1, base + 1)
        m0, l0, o0 = group(base, ql, qr)
        m1, l1, o1 = group(base + 1, ql, qr)

        m_prev = m_scr[...]
        m_new = jnp.maximum(m_prev, jnp.maximum(m0, m1))
        a_prev = jnp.exp2(m_prev - m_new)
        a0 = jnp.exp2(m0 - m_new)
        a1 = jnp.exp2(m1 - m_new)
        l_scr[...] = a_prev * l_scr[...] + a0 * l0 + a1 * l1
        acc_scr[...] = lanes(a_prev, 2) * acc_scr[...] + lanes(a0, 2) * o0 + lanes(a1, 2) * o1
        m_scr[...] = m_new

        @pl.when((2 * i + 2) % groups == 0)
        def _():
            o_ref[b] = acc_scr[...] / lanes(l_scr[...], 2)

        return carry

    lax.fori_loop(0, n_iter, body, 0)


def _decode_attn(page_table, q_lat, q_rope, c_new, kr_new, cache_latent, cache_krope_t):
    b, n_pages = page_table.shape
    groups = n_pages // DEC_PAGES
    assert n_pages % DEC_PAGES == 0 and groups % 2 == 0
    vmem = pl.BlockSpec(memory_space=pltpu.VMEM)
    return pl.pallas_call(
        functools.partial(_decode_kernel, groups=groups),
        in_specs=[pl.BlockSpec(memory_space=pltpu.SMEM), vmem, vmem, vmem, vmem,
                  pl.BlockSpec(memory_space=pl.ANY), pl.BlockSpec(memory_space=pl.ANY)],
        out_specs=vmem,
        out_shape=jax.ShapeDtypeStruct((b, N_HEADS, KV_LORA), F32),
        scratch_shapes=[
            pltpu.VMEM((DEC_SLOTS, DEC_PAGES * PAGE_SIZE, KV_LORA), F32),
            pltpu.VMEM((DEC_SLOTS, DEC_PAGES, QK_ROPE, PAGE_SIZE), F32),
            pltpu.SemaphoreType.DMA((DEC_SLOTS,)),
            pltpu.SemaphoreType.DMA((DEC_SLOTS,)),
            pltpu.VMEM((N_HEADS, HEAD_PAD), F32),
            pltpu.VMEM((N_HEADS, HEAD_PAD), F32),
            pltpu.VMEM((N_HEADS, KV_LORA), F32),
        ],
        compiler_params=pltpu.CompilerParams(vmem_limit_bytes=VMEM_LIMIT),
        name="decode_attn",
    )(page_table, q_lat, q_rope, c_new, kr_new, cache_latent, cache_krope_t)


def _olat_kernel(o_ref, w_ref, out_ref):
    out_ref[...] = (_dot(o_ref[0].astype(BF16), w_ref[0]) + _dot(o_ref[1].astype(BF16), w_ref[1])).astype(BF16)


def _o_latent_proj(o_lat, w_uv_pairs):
    m = o_lat.shape[1]
    return pl.pallas_call(
        _olat_kernel,
        grid=(N_HEADS // 2,),
        in_specs=[pl.BlockSpec((2, m, KV_LORA), lambda h: (h, 0, 0)),
                  pl.BlockSpec((2, KV_LORA, LANES), lambda h: (h, 0, 0))],
        out_specs=pl.BlockSpec((m, LANES), lambda h: (0, h)),
        out_shape=jax.ShapeDtypeStruct((m, N_HEADS * V_HEAD), BF16),
        compiler_params=_params("parallel"),
        name="o_latent_proj",
    )(o_lat, w_uv_pairs)


def _rope_tables(pos):
    half = QK_ROPE // 2
    inv = jnp.power(ROPE_BASE, -jnp.arange(half, dtype=F32) / half)
    ang = pos.astype(F32)[:, None] * inv[None, :]
    cos = jnp.cos(ang)
    sin = jnp.sin(ang)
    cos32 = jnp.concatenate([cos, cos], axis=1)
    sin32 = jnp.concatenate([sin, sin], axis=1)
    q_tab = Q_SCALE * jnp.concatenate([jnp.ones((pos.shape[0], QK_NOPE), F32), cos32, sin32], axis=1)
    return cos32, sin32, q_tab


def _rot_half_cols(w):
    half = QK_ROPE // 2
    return jnp.concatenate([-w[..., half:], w[..., :half]], axis=-1)


def _pad_heads(w, width):
    r = w.shape[0]
    return jnp.pad(w, ((0, 0), (0, 0), (0, HEAD_PAD - width))).reshape(r, HP)


def kernel(x_prompt, x_sample, cache_latent, cache_krope, state_conv, page_table, meta_tokens, a_norm_g, a_pw1_w, a_pw1_b, a_dw_w, a_dw_b, a_ln_g, a_ln_b, a_pw2_w, a_pw2_b, ffn_norm_g, ffn_w_gate, ffn_w_up, ffn_w_down, kv_norm_g, w_dkv, kv_latent_norm_g, w_uk, w_uv, b_norm_g, b_w_dq, b_q_norm_g, b_w_uq, b_w_o, final_norm_g):
    n_a = a_norm_g.shape[0]
    depth = ffn_norm_g.shape[0]
    n_b = depth - n_a
    bp, seq, _ = x_prompt.shape
    t_p = seq + N_META
    bs = x_sample.shape[0]
    past_len = page_table.shape[1] * PAGE_SIZE

    row = lambda v: v.reshape(1, -1).astype(F32)

    pw1_w = a_pw1_w.astype(BF16)
    pw2_w = a_pw2_w.astype(BF16)
    w_gate = ffn_w_gate.astype(BF16)
    w_up = ffn_w_up.astype(BF16)
    w_down = ffn_w_down.astype(BF16)
    first_tap = HIST - (CONV_W - 1)
    dw_tiles = jnp.pad(a_dw_w, ((0, 0), (first_tap, HIST + 8 - first_tap - CONV_W), (0, 0)))
    dw_tiles = dw_tiles.reshape(n_a, HIST + 8, 8, LANES)
    w_ckr = jnp.concatenate([w_dkv, _rot_half_cols(w_dkv[:, KV_LORA:])], axis=1).astype(BF16)
    w_k_pad = _pad_heads(w_uk, QK_NOPE).astype(BF16)
    w_v = w_uv.reshape(KV_LORA, N_HEADS * V_HEAD).astype(BF16)
    eye = jnp.eye(QK_ROPE, dtype=BF16)
    place = jnp.concatenate([jnp.zeros((QK_ROPE, QK_NOPE), BF16), eye, eye], axis=1)
    w_dq = b_w_dq.astype(BF16)
    w_q_pad = jnp.stack([
        _pad_heads(jnp.concatenate([b_w_uq[j], _rot_half_cols(b_w_uq[j][..., QK_NOPE:])], axis=-1), HEAD_PAD)
        for j in range(n_b)]).astype(BF16)
    w_o = b_w_o.astype(BF16)
    w_uk_t = jnp.pad(jnp.transpose(w_uk, (1, 2, 0)), ((0, 0), (0, HEAD_PAD - QK_NOPE), (0, 0))).astype(BF16)
    w_uv_h = jnp.transpose(w_uv, (1, 0, 2))
    w_uv_pairs = jnp.where((jnp.arange(N_HEADS) % 2 == 0)[:, None, None],
                           jnp.pad(w_uv_h, ((0, 0), (0, 0), (0, V_HEAD))),
                           jnp.pad(w_uv_h, ((0, 0), (0, 0), (V_HEAD, 0)))).astype(BF16)

    def ffn(x, l, tm, attn=None, window=None):
        final = row(final_norm_g) if l == depth - 1 else None
        return _ffn(x, row(ffn_norm_g[l]), w_gate, w_up, w_down, l, tm, attn=attn, final_g=final, window=window)

    def kv_args():
        return (row(kv_norm_g), w_ckr, row(kv_latent_norm_g))

    def q_args(j):
        return (row(b_norm_g[j]), w_dq, row(b_q_norm_g[j]), w_q_pad, j)

    xp = jnp.concatenate([jnp.broadcast_to(meta_tokens[None].astype(F32), (bp, N_META, D_MODEL)), x_prompt], axis=1)
    cos32, sin32, q_tab = _rope_tables(jnp.arange(t_p))
    m_p = bp * t_p
    t_pad = pl.cdiv(t_p, KV_CHUNK) * KV_CHUNK
    x = xp.reshape(m_p, D_MODEL)
    conv_p = []
    for l in range(n_a):
        glu = _pw1_glu(x, row(a_norm_g[l]), pw1_w, row(a_pw1_b[l]), PW1_TILE, tiles=True, layer=l)
        glu = glu.reshape(bp, t_p, 8, LANES)
        conv_p.append(glu[:, t_p - (CONV_W - 1):].reshape(bp, CONV_W - 1, D_MODEL))
        x = _conv_block(glu, x.reshape(bp, t_p, D_MODEL), dw_tiles[l], a_dw_b[l].reshape(8, LANES),
                        row(a_ln_g[l]), row(a_ln_b[l]), pw2_w, row(a_pw2_b[l]), SEQ_TILE, layer=l).reshape(m_p, D_MODEL)
        x = ffn(x, l, ROW_TILE)
    c_p, kr_p, k_p, v_p = _kv_side(x.reshape(bp, t_p, D_MODEL), *kv_args(), cos32, sin32, SEQ_TILE,
                                   heads=(w_k_pad, w_v, place), t_pad=t_pad)
    for j in range(n_b):
        q = _q_proj(x.reshape(bp, t_p, D_MODEL), *q_args(j), q_tab, SEQ_TILE)
        o = _attn_prompt(q, k_p, v_p).reshape(m_p, N_HEADS * V_HEAD)
        if n_a + j < depth - 1:
            x = ffn(x, n_a + j, ROW_TILE, attn=(o, w_o, j))
        else:
            x = ffn(x, n_a + j, OUT_TILE, attn=(o, w_o, j), window=(bp, t_p, N_META))
    y_prompt = x.reshape(bp, seq, D_MODEL)

    pos_s = jnp.full((bs,), past_len, jnp.int32)
    cos32s, sin32s, q_tab_s = _rope_tables(pos_s)
    x = x_sample.reshape(bs, D_MODEL)
    conv_s = []
    state_t = jnp.transpose(state_conv, (0, 2, 1, 3))
    krope_t = jnp.swapaxes(cache_krope, 1, 2)
    for l in range(n_a):
        glu = _pw1_glu(x, row(a_norm_g[l]), pw1_w, row(a_pw1_b[l]), bs, tiles=False, layer=l)
        conv_s.append(jnp.concatenate([state_conv[l][:, 1:], glu[:, None, :]], axis=1))
        x = _conv_step(state_t[l], glu, x, a_dw_w[l], row(a_dw_b[l]),
                       row(a_ln_g[l]), row(a_ln_b[l]), pw2_w[l], row(a_pw2_b[l]))
        x = ffn(x, l, bs)
    c_s, kr_s = _kv_side(x[None], *kv_args(), cos32s, sin32s, bs)
    c_s = c_s.reshape(bs, 1, KV_LORA)
    kr_s = kr_s.reshape(bs, 1, QK_ROPE)
    for j in range(n_b):
        q = _q_proj(x[None], *q_args(j), q_tab_s, bs)[0]
        q_lat, q_rope = _q_latent(q, w_uk_t)
        o_lat = _decode_attn(page_table, jnp.transpose(q_lat, (1, 0, 2)), jnp.transpose(q_rope, (1, 0, 2)),
                             c_s, kr_s, cache_latent, krope_t)
        o = _o_latent_proj(jnp.transpose(o_lat, (1, 0, 2)), w_uv_pairs)
        x = ffn(x, n_a + j, bs, attn=(o, w_o, j))
    y_sample = x.reshape(bs, 1, D_MODEL)

    return (y_prompt, y_sample, c_p, kr_p, jnp.stack(conv_p), c_s, kr_s, jnp.stack(conv_s))
```

```python
import functools
import math

import jax
import jax.numpy as jnp
from jax import lax
from jax.experimental import pallas as pl
from jax.experimental.pallas import tpu as pltpu

D_MODEL = 1024
N_HEADS = 16
QK_NOPE = 64
QK_ROPE = 32
V_HEAD = 64
KV_LORA = 256
Q_LORA = 384
CONV_W = 31
N_META = 16
PAGE_SIZE = 128
ROPE_BASE = 10000.0
EPS = 1e-6
SCALE = 1.0 / math.sqrt(QK_NOPE + QK_ROPE)
Q_SCALE = SCALE * math.log2(math.e)

LANES = 128
HEAD_PAD = LANES
HP = N_HEADS * HEAD_PAD

BF16 = jnp.bfloat16
F32 = jnp.float32

VMEM_LIMIT = 56 * 1024 * 1024

ROW_TILE = 384
SEQ_TILE = 688
OUT_TILE = 512
Q_TILE = 256
KV_CHUNK = 256
FFN_CHUNK = 256
PW1_TILE = 1376
PW1_PARTS = 2
PW1_CHUNK = 256
CONV_STEPS = 16
FUSED_TILE = 344
FUSED_GROUP = 2
HIST = 32
DEC_PAGES = 32
DEC_SLOTS = 4


def _params(*sem):
    return pltpu.CompilerParams(dimension_semantics=sem, vmem_limit_bytes=VMEM_LIMIT)


def _const_spec(shape, layer=None):
    n = len(shape)
    if layer is None:
        return pl.BlockSpec(shape, lambda *_: (0,) * n, pipeline_mode=pl.Buffered(1))
    return pl.BlockSpec((None, *shape), lambda *_: (layer,) + (0,) * n, pipeline_mode=pl.Buffered(1))


def _rms(x, g):
    return x * lax.rsqrt(jnp.mean(x * x, axis=-1, keepdims=True) + EPS) * g


def _sigmoid(x):
    return 1.0 / (1.0 + jnp.exp(-x))


def _dot(a, b):
    return jnp.dot(a, b, preferred_element_type=F32)


def _dot_t(a, b):
    return lax.dot_general(a, b, (((1,), (1,)), ((), ())), preferred_element_type=F32)


def _swap8(cols):
    cols = list(cols)
    sub = lax.broadcasted_iota(jnp.int32, cols[0].shape, 1)
    for d in (4, 2, 1):
        keep = (sub & d) == 0
        for j in range(8):
            if j & d:
                continue
            a, b = cols[j], cols[j + d]
            cols[j] = jnp.where(keep, a, pltpu.roll(b, d, axis=1))
            cols[j + d] = jnp.where(keep, pltpu.roll(a, 8 - d, axis=1), b)
    return cols


def _pw1_kernel(x_ref, g_ref, w_ref, b_ref, o_ref, *, tiles, parts):
    rows = x_ref.shape[0] // parts
    for part in range(parts):
        r0 = part * rows
        h = _rms(x_ref[r0:r0 + rows, :], g_ref[...]).astype(BF16)
        glu = []
        for c0 in range(0, D_MODEL, PW1_CHUNK):
            val = _dot(h, w_ref[:, c0:c0 + PW1_CHUNK]) + b_ref[:, c0:c0 + PW1_CHUNK]
            gate = (_dot(h, w_ref[:, D_MODEL + c0:D_MODEL + c0 + PW1_CHUNK])
                    + b_ref[:, D_MODEL + c0:D_MODEL + c0 + PW1_CHUNK])
            glu.append(val * _sigmoid(gate))
        if not tiles:
            o_ref[r0:r0 + rows, :] = jnp.concatenate(glu, axis=1)
            continue
        per = PW1_CHUNK // LANES
        cols = _swap8([glu[j // per][:, LANES * (j % per):LANES * (j % per + 1)].reshape(rows // 8, 8, LANES)
                       for j in range(8)])
        for t in range(8):
            o_ref[r0 // 8:(r0 + rows) // 8, t] = cols[t]


def _pw1_glu(x, g, w, b, tm, tiles, layer):
    m = x.shape[0]
    if tiles:
        out_spec = pl.BlockSpec((tm // 8, 8, 8, LANES), lambda i: (i, 0, 0, 0))
        out_shape = jax.ShapeDtypeStruct((m // 8, 8, 8, LANES), F32)
    else:
        out_spec = pl.BlockSpec((tm, D_MODEL), lambda i: (i, 0))
        out_shape = jax.ShapeDtypeStruct((m, D_MODEL), F32)
    return pl.pallas_call(
        functools.partial(_pw1_kernel, tiles=tiles, parts=PW1_PARTS if tm % (8 * PW1_PARTS) == 0 and tiles else 1),
        grid=(m // tm,),
        in_specs=[
            pl.BlockSpec((tm, D_MODEL), lambda i: (i, 0)),
            _const_spec((1, D_MODEL)),
            _const_spec((D_MODEL, 2 * D_MODEL), layer),
            _const_spec((1, 2 * D_MODEL)),
        ],
        out_specs=out_spec,
        out_shape=out_shape,
        compiler_params=_params("parallel"),
        name="pw1_glu",
    )(x, g, w, b)


def _ln_swish_pw2(y, x, lng_ref, lnb_ref, w2_ref, b2_ref):
    mu = jnp.mean(y, axis=-1, keepdims=True)
    yc = y - mu
    yn = yc * lax.rsqrt(jnp.mean(yc * yc, axis=-1, keepdims=True) + EPS)
    yn = yn * lng_ref[...] + lnb_ref[...]
    z = (yn * _sigmoid(yn)).astype(BF16)
    return x + _dot(z, w2_ref[...]) + b2_ref[...]


def _conv_kernel(g_ref, x_ref, dw_ref, dwb_ref, lng_ref, lnb_ref, w2_ref, b2_ref, o_ref,
                 buf, y_scr, *, tt):
    t = pl.program_id(1)

    @pl.when(t == 0)
    def _():
        buf[0:HIST] = jnp.zeros((HIST, 8, LANES), F32)

    @pl.when(t > 0)
    def _():
        buf[0:HIST] = buf[tt:tt + HIST]

    buf[HIST:HIST + tt] = g_ref[...]
    bias = jnp.broadcast_to(dwb_ref[...], (CONV_STEPS, 8, LANES))

    def block(i, carry):
        rs = i * CONV_STEPS
        acc = bias
        for o in range(HIST - (CONV_W - 1), HIST + 1):
            acc = acc + buf[pl.ds(rs + o, CONV_STEPS)] * dw_ref[o]
        y_scr[pl.ds(i * (CONV_STEPS // 8), CONV_STEPS // 8)] = acc.reshape(CONV_STEPS // 8, 8, 8, LANES)
        return carry

    lax.fori_loop(0, tt // CONV_STEPS, block, 0)
    cols = _swap8([y_scr[:, s] for s in range(8)])
    y = jnp.concatenate([c.reshape(tt, LANES) for c in cols], axis=1)
    o_ref[...] = _ln_swish_pw2(y, x_ref[...], lng_ref, lnb_ref, w2_ref, b2_ref)


def _conv_block(glu, x, dw_tiles, dwb_tile, lng, lnb, w2, b2, tt, layer):
    b, t, _ = x.shape
    assert tt % CONV_STEPS == 0 and t % tt == 0
    rows = pl.BlockSpec((None, tt, D_MODEL), lambda bi, ti: (bi, ti, 0))
    return pl.pallas_call(
        functools.partial(_conv_kernel, tt=tt),
        grid=(b, t // tt),
        in_specs=[
            pl.BlockSpec((None, tt, 8, LANES), lambda bi, ti: (bi, ti, 0, 0)),
            rows,
            _const_spec((HIST + 8, 8, LANES)),
            _const_spec((8, LANES)),
            _const_spec((1, D_MODEL)),
            _const_spec((1, D_MODEL)),
            _const_spec((D_MODEL, D_MODEL), layer),
            _const_spec((1, D_MODEL)),
        ],
        out_specs=rows,
        out_shape=jax.ShapeDtypeStruct((b, t, D_MODEL), F32),
        scratch_shapes=[
            pltpu.VMEM((HIST + tt, 8, LANES), F32),
            pltpu.VMEM((tt // 8, 8, 8, LANES), F32),
        ],
        compiler_params=_params("parallel", "arbitrary"),
        name="conv_block",
    )(glu, x, dw_tiles, dwb_tile, lng, lnb, w2, b2)


def _conv_step_kernel(st_ref, g_ref, x_ref, dw_ref, dwb_ref, lng_ref, lnb_ref, w2_ref, b2_ref, o_ref):
    g = g_ref[...]
    y = g * dw_ref[CONV_W - 1:CONV_W, :] + dwb_ref[...]
    for k in range(CONV_W - 1):
        y = y + st_ref[k] * dw_ref[k:k + 1, :]
    o_ref[...] = _ln_swish_pw2(y, x_ref[...], lng_ref, lnb_ref, w2_ref, b2_ref)


def _conv_step(state, glu, x, dw, dwb, lng, lnb, w2, b2):
    m = x.shape[0]
    return pl.pallas_call(
        _conv_step_kernel,
        out_shape=jax.ShapeDtypeStruct((m, D_MODEL), F32),
        compiler_params=pltpu.CompilerParams(vmem_limit_bytes=VMEM_LIMIT),
        name="conv_step",
    )(state, glu, x, dw, dwb, lng, lnb, w2, b2)


def _ffn_kernel(*refs, pre, final, d_ff):
    it = iter(refs)
    x_ref = next(it)
    if pre:
        o_ref, wo_ref = next(it), next(it)
    g_ref, wg_ref, wu_ref, wd_ref = next(it), next(it), next(it), next(it)
    if final:
        gf_ref = next(it)
    out_ref, a_scr = next(it), next(it)

    x = x_ref[...]
    if pre:
        x = x + _dot(o_ref[...], wo_ref[...])
    h = _rms(x, g_ref[...]).astype(BF16)
    for c in range(d_ff // FFN_CHUNK):
        sl = slice(c * FFN_CHUNK, (c + 1) * FFN_CHUNK)
        gate = _dot(h, wg_ref[:, sl])
        up = _dot(h, wu_ref[:, sl])
        a_scr[:, sl] = (gate * _sigmoid(gate) * up).astype(BF16)
    y = x + _dot(a_scr[...], wd_ref[...])
    if final:
        y = _rms(y, gf_ref[...])
    out_ref[...] = y


def _ffn(x, g, wg, wu, wd, layer, tm, attn=None, final_g=None, window=None):
    m = x.shape[0]
    d_ff = wg.shape[2]
    if window is None:
        grid = (m // tm,)
        m_out = m
        rows_in = lambda width: pl.BlockSpec((tm, width), lambda i: (i, 0))
        rows_out = pl.BlockSpec((tm, D_MODEL), lambda i: (i, 0))
    else:
        n_seq, t_in, t_skip = window
        per_seq = (t_in - t_skip) // tm
        assert per_seq * tm == t_in - t_skip and t_in % 8 == 0 and t_skip % 8 == 0 and tm % 8 == 0
        grid = (n_seq, per_seq)
        m_out = n_seq * per_seq * tm
        first_row = lambda b, i: pl.multiple_of(b * t_in + t_skip + i * tm, 8)
        rows_in = lambda width: pl.BlockSpec((pl.Element(tm), pl.Element(width)), lambda b, i: (first_row(b, i), 0))
        rows_out = pl.BlockSpec((tm, D_MODEL), lambda b, i: (b * per_seq + i, 0))
    args = [x]
    specs = [rows_in(D_MODEL)]
    if attn is not None:
        o, wo, j = attn
        args += [o, wo]
        specs += [rows_in(o.shape[1]), _const_spec(wo.shape[1:], j)]
    args += [g, wg, wu, wd]
    specs += [_const_spec((1, D_MODEL)), _const_spec((D_MODEL, d_ff), layer),
              _const_spec((D_MODEL, d_ff), layer), _const_spec((d_ff, D_MODEL), layer)]
    if final_g is not None:
        args.append(final_g)
        specs.append(_const_spec((1, D_MODEL)))
    return pl.pallas_call(
        functools.partial(_ffn_kernel, pre=attn is not None, final=final_g is not None, d_ff=d_ff),
        grid=grid,
        in_specs=specs,
        out_specs=rows_out,
        out_shape=jax.ShapeDtypeStruct((m_out, D_MODEL), F32),
        scratch_shapes=[pltpu.VMEM((tm, d_ff), BF16)],
        compiler_params=_params(*(["parallel"] * len(grid))),
        name="ffn",
    )(*args)


def _conv_ffn_kernel(g_ref, x_ref, dw_ref, dwb_ref, lng_ref, lnb_ref, w2_ref, b2_ref,
                     fg_ref, wg_ref, wu_ref, wd_ref, o_ref, buf, y_scr, mid_scr, a_scr,
                     *, tt, tiles_per_seq, n_tiles, d_ff):
    s = pl.program_id(0)
    tile = jnp.minimum(s, n_tiles - 1)
    first_of_seq = (tile % tiles_per_seq) == 0
    prev = (s + 1) % 2

    @pl.when(s == 0)
    def _():
        mid_scr[...] = jnp.zeros_like(mid_scr)
        buf[tt:] = jnp.zeros((buf.shape[0] - tt, 8, LANES), F32)

    h = _rms(mid_scr[prev], fg_ref[...]).astype(BF16)

    buf[0:HIST] = jnp.where(first_of_seq, 0.0, buf[tt:tt + HIST])
    buf[HIST:HIST + tt] = g_ref[...]
    bias = jnp.broadcast_to(dwb_ref[...], (8, 8, LANES))
    n_ffn = d_ff // FFN_CHUNK
    per_chunk = pl.cdiv(tt // 8, n_ffn)

    def chunk(c):
        lo = c * FFN_CHUNK if isinstance(c, int) else pl.multiple_of(c * FFN_CHUNK, FFN_CHUNK)
        gate = _dot(h, wg_ref[:, pl.ds(lo, FFN_CHUNK)])
        up = _dot(h, wu_ref[:, pl.ds(lo, FFN_CHUNK)])
        a_scr[:, pl.ds(lo, FFN_CHUNK)] = (gate * _sigmoid(gate) * up).astype(BF16)
        for i in range(per_chunk):
            grp = c * per_chunk + i
            rs = grp * 8
            acc = bias
            for o in range(HIST - (CONV_W - 1), HIST + 1):
                acc = acc + buf[pl.ds(rs + o, 8)] * dw_ref[o]
            y_scr[grp] = acc

    def chunk_group(i, carry):
        for k in range(FUSED_GROUP):
            chunk(FUSED_GROUP * i + k)
        return carry

    lax.fori_loop(0, n_ffn // FUSED_GROUP, chunk_group, 0)
    for c in range(n_ffn // FUSED_GROUP * FUSED_GROUP, n_ffn):
        chunk(c)

    cols = _swap8([y_scr[0:tt // 8, t] for t in range(8)])
    y = jnp.concatenate([c.reshape(tt, LANES) for c in cols], axis=1)
    mid_scr[s % 2] = _ln_swish_pw2(y, x_ref[...], lng_ref, lnb_ref, w2_ref, b2_ref)
    o_ref[...] = mid_scr[prev] + _dot(a_scr[...], wd_ref[...])


def _conv_ffn(glu, x, dw_tiles, dwb_tile, lng, lnb, w2, b2, fg, wg, wu, wd, tt, conv_layer, ffn_layer):
    b, t, _ = x.shape
    d_ff = wg.shape[2]
    assert t % tt == 0 and tt % 8 == 0 and d_ff % FFN_CHUNK == 0
    n_groups = pl.cdiv(tt // 8, d_ff // FFN_CHUNK) * (d_ff // FFN_CHUNK)
    tps = t // tt
    n_tiles = b * tps
    cur = lambda s: jnp.minimum(s, n_tiles - 1)
    rows_in = pl.BlockSpec((None, tt, D_MODEL), lambda s: (cur(s) // tps, cur(s) % tps, 0))
    rows_out = pl.BlockSpec((None, tt, D_MODEL),
                            lambda s: (jnp.maximum(s - 1, 0) // tps, jnp.maximum(s - 1, 0) % tps, 0))
    return pl.pallas_call(
        functools.partial(_conv_ffn_kernel, tt=tt, tiles_per_seq=tps, n_tiles=n_tiles, d_ff=d_ff),
        grid=(n_tiles + 1,),
        in_specs=[
            pl.BlockSpec((None, tt, 8, LANES), lambda s: (cur(s) // tps, cur(s) % tps, 0, 0)),
            rows_in,
            _const_spec((HIST + 8, 8, LANES)),
            _const_spec((8, LANES)),
            _const_spec((1, D_MODEL)),
            _const_spec((1, D_MODEL)),
            _const_spec((D_MODEL, D_MODEL), conv_layer),
            _const_spec((1, D_MODEL)),
            _const_spec((1, D_MODEL)),
            _const_spec((D_MODEL, d_ff), ffn_layer),
            _const_spec((D_MODEL, d_ff), ffn_layer),
            _const_spec((d_ff, D_MODEL), ffn_layer),
        ],
        out_specs=rows_out,
        out_shape=jax.ShapeDtypeStruct((b, t, D_MODEL), F32),
        scratch_shapes=[
            pltpu.VMEM((HIST + 8 * n_groups, 8, LANES), F32),
            pltpu.VMEM((n_groups, 8, 8, LANES), F32),
            pltpu.VMEM((2, tt, D_MODEL), F32),
            pltpu.VMEM((tt, d_ff), BF16),
        ],
        compiler_params=_params("arbitrary"),
        name="conv_ffn",
    )(glu, x, dw_tiles, dwb_tile, lng, lnb, w2, b2, fg, wg, wu, wd)


def _kv_kernel(*refs, heads, n_real):
    if heads:
        (x_ref, g_ref, w_ref, gl_ref, cos_ref, sin_ref, wk_ref, wv_ref, e_ref,
         c_ref, kr_ref, k_ref, v_ref) = refs
    else:
        x_ref, g_ref, w_ref, gl_ref, cos_ref, sin_ref, c_ref, kr_ref = refs

    def real():
        h = _rms(x_ref[...], g_ref[...]).astype(BF16)
        ckr = _dot(h, w_ref[...])
        c = _rms(ckr[:, :KV_LORA], gl_ref[...])
        kr = (ckr[:, KV_LORA:KV_LORA + QK_ROPE] * cos_ref[...]
              + ckr[:, KV_LORA + QK_ROPE:] * sin_ref[...])
        c_ref[...] = c
        kr_ref[...] = kr
        if heads:
            cb = c.astype(BF16)
            k_nope = _dot(cb, wk_ref[...])
            k_rope = _dot(kr.astype(BF16), e_ref[...])
            for hd in range(N_HEADS):
                sl = slice(hd * HEAD_PAD, (hd + 1) * HEAD_PAD)
                k_ref[:, sl] = (k_nope[:, sl] + k_rope).astype(BF16)
            v_ref[...] = _dot(cb, wv_ref[...]).astype(BF16)

    if not heads:
        real()
        return

    t = pl.program_id(1)
    pl.when(t < n_real)(real)

    @pl.when(t >= n_real)
    def _():
        k_ref[...] = jnp.zeros_like(k_ref)
        v_ref[...] = jnp.zeros_like(v_ref)


def _kv_side(x, g, w, gl, cos, sin, tt, heads=None, t_pad=None):
    b, t, _ = x.shape
    n_real = t // tt
    n_steps = n_real if heads is None else pl.cdiv(t_pad, tt)
    clamp = lambda bi, ti: (bi, jnp.minimum(ti, n_real - 1), 0)
    tab = lambda bi, ti: (jnp.minimum(ti, n_real - 1), 0)
    args = [x, g, w, gl, cos, sin]
    specs = [
        pl.BlockSpec((None, tt, D_MODEL), clamp),
        _const_spec((1, D_MODEL)),
        _const_spec((D_MODEL, KV_LORA + 2 * QK_ROPE)),
        _const_spec((1, KV_LORA)),
        pl.BlockSpec((tt, QK_ROPE), tab),
        pl.BlockSpec((tt, QK_ROPE), tab),
    ]
    out_specs = [pl.BlockSpec((None, tt, KV_LORA), clamp), pl.BlockSpec((None, tt, QK_ROPE), clamp)]
    out_shape = [jax.ShapeDtypeStruct((b, t, KV_LORA), F32), jax.ShapeDtypeStruct((b, t, QK_ROPE), F32)]
    if heads is not None:
        args += list(heads)
        hv = N_HEADS * V_HEAD
        specs += [_const_spec((KV_LORA, HP)), _const_spec((KV_LORA, hv)), _const_spec((QK_ROPE, HEAD_PAD))]
        full = lambda bi, ti: (bi, ti, 0)
        out_specs += [pl.BlockSpec((None, tt, HP), full), pl.BlockSpec((None, tt, hv), full)]
        out_shape += [jax.ShapeDtypeStruct((b, t_pad, HP), BF16), jax.ShapeDtypeStruct((b, t_pad, hv), BF16)]
    return pl.pallas_call(
        functools.partial(_kv_kernel, heads=heads is not None, n_real=n_real),
        grid=(b, n_steps),
        in_specs=specs,
        out_specs=out_specs,
        out_shape=out_shape,
        compiler_params=_params("parallel", "arbitrary"),
        name="kv_side",
    )(*args)


def _q_kernel(x_ref, g_ref, wdq_ref, gq_ref, wq_ref, tab_ref, q_ref):
    h = _rms(x_ref[...], g_ref[...]).astype(BF16)
    cq = _rms(_dot(h, wdq_ref[...]), gq_ref[...]).astype(BF16)
    q = _dot(cq, wq_ref[...])
    tab = tab_ref[...]
    for hd in range(N_HEADS):
        sl = slice(hd * HEAD_PAD, (hd + 1) * HEAD_PAD)
        q_ref[:, sl] = (q[:, sl] * tab).astype(BF16)


def _q_proj(x, g, wdq, gq, wq, layer, tab, tt):
    b, t, _ = x.shape
    full = lambda bi, ti: (bi, ti, 0)
    return pl.pallas_call(
        _q_kernel,
        grid=(b, t // tt),
        in_specs=[
            pl.BlockSpec((None, tt, D_MODEL), full),
            _const_spec((1, D_MODEL)),
            _const_spec((D_MODEL, Q_LORA), layer),
            _const_spec((1, Q_LORA)),
            _const_spec((Q_LORA, HP), layer),
            pl.BlockSpec((tt, HEAD_PAD), lambda bi, ti: (ti, 0)),
        ],
        out_specs=pl.BlockSpec((None, tt, HP), full),
        out_shape=jax.ShapeDtypeStruct((b, t, HP), BF16),
        compiler_params=_params("parallel", "parallel"),
        name="q_proj",
    )(x, g, wdq, gq, wq, tab)


def _query_tiles(t):
    tiles = [(q0, min(Q_TILE, t - q0)) for q0 in range(0, t, Q_TILE)]
    if len(tiles) > 1 and tiles[-1][1] < Q_TILE // 2:
        (q0, rows), (_, extra) = tiles[-2:]
        tiles[-2:] = [(q0, rows + extra)]
    return tiles


def _attn_kernel(q_ref, k_ref, v_ref, o_ref, s_scr, p_scr, *, t):
    half = KV_CHUNK // 2
    work = [(q0, rows, e) for q0, rows in _query_tiles(t) for e in range(2)]

    def chunks(q0, rows):
        return range(0, pl.cdiv(q0 + rows, KV_CHUNK) * KV_CHUNK, KV_CHUNK)

    def scores(w, q0, rows, e):
        lanes = slice(e * HEAD_PAD, (e + 1) * HEAD_PAD)
        s_buf = s_scr.at[w % 2]
        q = q_ref[q0:q0 + rows, lanes]
        mx = None
        for ks in chunks(q0, rows):
            s = _dot_t(q, k_ref[ks:ks + KV_CHUNK, lanes])
            if ks + KV_CHUNK - 1 > q0:
                qpos = q0 + lax.broadcasted_iota(jnp.int32, s.shape, 0)
                kpos = ks + lax.broadcasted_iota(jnp.int32, s.shape, 1)
                s = jnp.where(kpos <= qpos, s, -jnp.inf)
            s_buf[0:rows, ks:ks + KV_CHUNK] = s
            cm = jnp.maximum(s[:, :half], s[:, half:])
            mx = cm if mx is None else jnp.maximum(mx, cm)
        return jnp.max(mx, axis=1, keepdims=True)

    def probs(w, q0, rows, e, m):
        s_buf = s_scr.at[w % 2]
        p_buf = p_scr.at[w % 2]
        sm = None
        for ks in chunks(q0, rows):
            p = jnp.exp2(s_buf[0:rows, ks:ks + KV_CHUNK] - m)
            p_buf[0:rows, ks:ks + KV_CHUNK] = p.astype(BF16)
            ps = p[:, :half] + p[:, half:]
            sm = ps if sm is None else sm + ps
        return jnp.sum(sm, axis=1, keepdims=True)

    def values(w, q0, rows, e, l):
        n_keys = chunks(q0, rows).stop
        return _dot(p_scr[w % 2, 0:rows, 0:n_keys], v_ref[0:n_keys, :]) / l

    m_of, l_of, held = {}, {}, None
    for step in range(len(work) + 2):
        if step < len(work):
            m_of[step] = scores(step, *work[step])
        if 0 <= step - 1 < len(work):
            l_of[step - 1] = probs(step - 1, *work[step - 1], m_of.pop(step - 1))
        if 0 <= step - 2 < len(work):
            q0, rows, e = work[step - 2]
            o = values(step - 2, q0, rows, e, l_of.pop(step - 2))
            if e == 0:
                held = o
            else:
                lane = lax.broadcasted_iota(jnp.int32, o.shape, 1)
                o_ref[q0:q0 + rows, :] = jnp.where(lane < V_HEAD, held, o).astype(BF16)


def _attn_prompt(q, k, v):
    b, t, _ = q.shape
    t_pad = k.shape[1]
    pair = 2 * HEAD_PAD
    tile_rows = max(rows for _, rows in _query_tiles(t))
    return pl.pallas_call(
        functools.partial(_attn_kernel, t=t),
        grid=(b, N_HEADS // 2),
        in_specs=[pl.BlockSpec((None, t, pair), lambda bi, hi: (bi, 0, hi)),
                  pl.BlockSpec((None, t_pad, pair), lambda bi, hi: (bi, 0, hi)),
                  pl.BlockSpec((None, t_pad, 2 * V_HEAD), lambda bi, hi: (bi, 0, hi))],
        out_specs=pl.BlockSpec((None, t, 2 * V_HEAD), lambda bi, hi: (bi, 0, hi)),
        out_shape=jax.ShapeDtypeStruct((b, t, N_HEADS * V_HEAD), BF16),
        scratch_shapes=[pltpu.VMEM((2, tile_rows, t_pad), F32), pltpu.VMEM((2, tile_rows, t_pad), BF16)],
        compiler_params=_params("parallel", "parallel"),
        name="attn_prompt",
    )(q, k, v)


def _qlat_kernel(q_ref, w_ref, o_ref, r_ref):
    q = q_ref[...]
    o_ref[...] = _dot(q, w_ref[...]).astype(BF16)
    rot = q[:, QK_NOPE:].astype(F32)
    r_ref[...] = (rot[:, :QK_ROPE] + rot[:, QK_ROPE:]).astype(BF16)


def _q_latent(q, w_uk_t):
    m = q.shape[0]
    return pl.pallas_call(
        _qlat_kernel,
        grid=(N_HEADS,),
        in_specs=[pl.BlockSpec((m, HEAD_PAD), lambda h: (0, h)),
                  pl.BlockSpec((None, HEAD_PAD, KV_LORA), lambda h: (h, 0, 0))],
        out_specs=[pl.BlockSpec((None, m, KV_LORA), lambda h: (h, 0, 0)),
                   pl.BlockSpec((None, m, QK_ROPE), lambda h: (h, 0, 0))],
        out_shape=[jax.ShapeDtypeStruct((N_HEADS, m, KV_LORA), BF16),
                   jax.ShapeDtypeStruct((N_HEADS, m, QK_ROPE), BF16)],
        compiler_params=_params("parallel"),
        name="q_latent",
    )(q, w_uk_t)


def _decode_kernel(pt_ref, ql_ref, qr_ref, cn_ref, krn_ref, lat_hbm, krt_hbm, o_ref,
                   lat_buf, kr_buf, lat_sem, kr_sem, m_scr, l_scr, acc_scr, *, groups):
    n_seq = ql_ref.shape[0]
    n_iter = n_seq * groups // 2
    lanes = lambda a, n: jnp.concatenate([a] * n, axis=1)

    def copies(t, slot):
        b = t // groups
        first_page = (t % groups) * DEC_PAGES
        out = []
        for j in range(DEC_PAGES):
            page = pt_ref[b, first_page + j]
            out.append(pltpu.make_async_copy(
                lat_hbm.at[page], lat_buf.at[slot, pl.ds(j * PAGE_SIZE, PAGE_SIZE)], lat_sem.at[slot]))
            out.append(pltpu.make_async_copy(krt_hbm.at[page], kr_buf.at[slot, j], kr_sem.at[slot]))
        return out

    def start(t, slot):
        for cp in copies(t, slot):
            cp.start()

    def wait(t, slot):
        for cp in copies(t, slot):
            cp.wait()

    def group(slot, ql, qr):
        c = lat_buf[slot].astype(BF16)
        kr_t = jnp.concatenate([kr_buf[slot, j] for j in range(DEC_PAGES)], axis=1).astype(BF16)
        s = _dot_t(ql, c) + _dot(qr, kr_t)
        m = jnp.max(s, axis=1, keepdims=True)
        p = jnp.exp2(s - m)
        return m, jnp.sum(p, axis=1, keepdims=True), _dot(p.astype(BF16), c)

    start(0, 0)
    start(1, 1)

    def body(i, carry):
        base = (i % 2) * 2
        b = (2 * i) // groups
        ql = ql_ref[b]
        qr = qr_ref[b]

        @pl.when(i + 1 < n_iter)
        def _():
            start(2 * i + 2, 2 - base)
            start(2 * i + 3, 3 - base)

        @pl.when((2 * i) % groups == 0)
        def _():
            cn = cn_ref[b].astype(BF16).astype(F32)
            krn = krn_ref[b].astype(BF16).astype(F32)
            s_new = (jnp.sum(ql.astype(F32) * cn, axis=1, keepdims=True)
                     + jnp.sum(qr.astype(F32) * krn, axis=1, keepdims=True))
            m_scr[...] = jnp.broadcast_to(s_new, m_scr.shape)
            l_scr[...] = jnp.ones_like(l_scr)
            acc_scr[...] = jnp.broadcast_to(cn, acc_scr.shape)

        wait(2 * i, base)
        wait(2 * i + 1, base + 1)
        m0, l0, o0 = group(base, ql, qr)
        m1, l1, o1 = group(base + 1, ql, qr)

        m_prev = m_scr[...]
        m_new = jnp.maximum(m_prev, jnp.maximum(m0, m1))
        a_prev = jnp.exp2(m_prev - m_new)
        a0 = jnp.exp2(m0 - m_new)
        a1 = jnp.exp2(m1 - m_new)
        l_scr[...] = a_prev * l_scr[...] + a0 * l0 + a1 * l1
        acc_scr[...] = lanes(a_prev, 2) * acc_scr[...] + lanes(a0, 2) * o0 + lanes(a1, 2) * o1
        m_scr[...] = m_new

        @pl.when((2 * i + 2) % groups == 0)
        def _():
            o_ref[b] = acc_scr[...] / lanes(l_scr[...], 2)

        return carry

    lax.fori_loop(0, n_iter, body, 0)


def _decode_attn(page_table, q_lat, q_rope, c_new, kr_new, cache_latent, cache_krope_t):
    b, n_pages = page_table.shape
    groups = n_pages // DEC_PAGES
    assert n_pages % DEC_PAGES == 0 and groups % 2 == 0
    vmem = pl.BlockSpec(memory_space=pltpu.VMEM)
    return pl.pallas_call(
        functools.partial(_decode_kernel, groups=groups),
        in_specs=[pl.BlockSpec(memory_space=pltpu.SMEM), vmem, vmem, vmem, vmem,
                  pl.BlockSpec(memory_space=pl.ANY), pl.BlockSpec(memory_space=pl.ANY)],
        out_specs=vmem,
        out_shape=jax.ShapeDtypeStruct((b, N_HEADS, KV_LORA), F32),
        scratch_shapes=[
            pltpu.VMEM((DEC_SLOTS, DEC_PAGES * PAGE_SIZE, KV_LORA), F32),
            pltpu.VMEM((DEC_SLOTS, DEC_PAGES, QK_ROPE, PAGE_SIZE), F32),
            pltpu.SemaphoreType.DMA((DEC_SLOTS,)),
            pltpu.SemaphoreType.DMA((DEC_SLOTS,)),
            pltpu.VMEM((N_HEADS, HEAD_PAD), F32),
            pltpu.VMEM((N_HEADS, HEAD_PAD), F32),
            pltpu.VMEM((N_HEADS, KV_LORA), F32),
        ],
        compiler_params=pltpu.CompilerParams(vmem_limit_bytes=VMEM_LIMIT),
        name="decode_attn",
    )(page_table, q_lat, q_rope, c_new, kr_new, cache_latent, cache_krope_t)


def _olat_kernel(o_ref, w_ref, out_ref):
    out_ref[...] = (_dot(o_ref[0].astype(BF16), w_ref[0]) + _dot(o_ref[1].astype(BF16), w_ref[1])).astype(BF16)


def _o_latent_proj(o_lat, w_uv_pairs):
    m = o_lat.shape[1]
    return pl.pallas_call(
        _olat_kernel,
        grid=(N_HEADS // 2,),
        in_specs=[pl.BlockSpec((2, m, KV_LORA), lambda h: (h, 0, 0)),
                  pl.BlockSpec((2, KV_LORA, LANES), lambda h: (h, 0, 0))],
        out_specs=pl.BlockSpec((m, LANES), lambda h: (0, h)),
        out_shape=jax.ShapeDtypeStruct((m, N_HEADS * V_HEAD), BF16),
        compiler_params=_params("parallel"),
        name="o_latent_proj",
    )(o_lat, w_uv_pairs)


def _rope_tables(pos):
    half = QK_ROPE // 2
    inv = jnp.power(ROPE_BASE, -jnp.arange(half, dtype=F32) / half)
    ang = pos.astype(F32)[:, None] * inv[None, :]
    cos = jnp.cos(ang)
    sin = jnp.sin(ang)
    cos32 = jnp.concatenate([cos, cos], axis=1)
    sin32 = jnp.concatenate([sin, sin], axis=1)
    q_tab = Q_SCALE * jnp.concatenate([jnp.ones((pos.shape[0], QK_NOPE), F32), cos32, sin32], axis=1)
    return cos32, sin32, q_tab


def _rot_half_cols(w):
    half = QK_ROPE // 2
    return jnp.concatenate([-w[..., half:], w[..., :half]], axis=-1)


def _pad_heads(w, width):
    r = w.shape[0]
    return jnp.pad(w, ((0, 0), (0, 0), (0, HEAD_PAD - width))).reshape(r, HP)


def kernel(x_prompt, x_sample, cache_latent, cache_krope, state_conv, page_table, meta_tokens, a_norm_g, a_pw1_w, a_pw1_b, a_dw_w, a_dw_b, a_ln_g, a_ln_b, a_pw2_w, a_pw2_b, ffn_norm_g, ffn_w_gate, ffn_w_up, ffn_w_down, kv_norm_g, w_dkv, kv_latent_norm_g, w_uk, w_uv, b_norm_g, b_w_dq, b_q_norm_g, b_w_uq, b_w_o, final_norm_g):
    n_a = a_norm_g.shape[0]
    depth = ffn_norm_g.shape[0]
    n_b = depth - n_a
    bp, seq, _ = x_prompt.shape
    t_p = seq + N_META
    bs = x_sample.shape[0]
    past_len = page_table.shape[1] * PAGE_SIZE

    row = lambda v: v.reshape(1, -1).astype(F32)

    pw1_w = a_pw1_w.astype(BF16)
    pw2_w = a_pw2_w.astype(BF16)
    w_gate = ffn_w_gate.astype(BF16)
    w_up = ffn_w_up.astype(BF16)
    w_down = ffn_w_down.astype(BF16)
    first_tap = HIST - (CONV_W - 1)
    dw_tiles = jnp.pad(a_dw_w, ((0, 0), (first_tap, HIST + 8 - first_tap - CONV_W), (0, 0)))
    dw_tiles = dw_tiles.reshape(n_a, HIST + 8, 8, LANES)
    w_ckr = jnp.concatenate([w_dkv, _rot_half_cols(w_dkv[:, KV_LORA:])], axis=1).astype(BF16)
    w_k_pad = _pad_heads(w_uk, QK_NOPE).astype(BF16)
    w_v = w_uv.reshape(KV_LORA, N_HEADS * V_HEAD).astype(BF16)
    eye = jnp.eye(QK_ROPE, dtype=BF16)
    place = jnp.concatenate([jnp.zeros((QK_ROPE, QK_NOPE), BF16), eye, eye], axis=1)
    w_dq = b_w_dq.astype(BF16)
    w_q_pad = jnp.stack([
        _pad_heads(jnp.concatenate([b_w_uq[j], _rot_half_cols(b_w_uq[j][..., QK_NOPE:])], axis=-1), HEAD_PAD)
        for j in range(n_b)]).astype(BF16)
    w_o = b_w_o.astype(BF16)
    w_uk_t = jnp.pad(jnp.transpose(w_uk, (1, 2, 0)), ((0, 0), (0, HEAD_PAD - QK_NOPE), (0, 0))).astype(BF16)
    w_uv_h = jnp.transpose(w_uv, (1, 0, 2))
    w_uv_pairs = jnp.where((jnp.arange(N_HEADS) % 2 == 0)[:, None, None],
                           jnp.pad(w_uv_h, ((0, 0), (0, 0), (0, V_HEAD))),
                           jnp.pad(w_uv_h, ((0, 0), (0, 0), (V_HEAD, 0)))).astype(BF16)

    def ffn(x, l, tm, attn=None, window=None):
        final = row(final_norm_g) if l == depth - 1 else None
        return _ffn(x, row(ffn_norm_g[l]), w_gate, w_up, w_down, l, tm, attn=attn, final_g=final, window=window)

    def kv_args():
        return (row(kv_norm_g), w_ckr, row(kv_latent_norm_g))

    def q_args(j):
        return (row(b_norm_g[j]), w_dq, row(b_q_norm_g[j]), w_q_pad, j)

    xp = jnp.concatenate([jnp.broadcast_to(meta_tokens[None].astype(F32), (bp, N_META, D_MODEL)), x_prompt], axis=1)
    cos32, sin32, q_tab = _rope_tables(jnp.arange(t_p))
    m_p = bp * t_p
    t_pad = pl.cdiv(t_p, KV_CHUNK) * KV_CHUNK
    x = xp.reshape(m_p, D_MODEL)
    conv_p = []
    for l in range(n_a):
        glu = _pw1_glu(x, row(a_norm_g[l]), pw1_w, row(a_pw1_b[l]), PW1_TILE, tiles=True, layer=l)
        glu = glu.reshape(bp, t_p, 8, LANES)
        conv_p.append(glu[:, t_p - (CONV_W - 1):].reshape(bp, CONV_W - 1, D_MODEL))
        x = _conv_ffn(glu, x.reshape(bp, t_p, D_MODEL), dw_tiles[l], a_dw_b[l].reshape(8, LANES),
                      row(a_ln_g[l]), row(a_ln_b[l]), pw2_w, row(a_pw2_b[l]),
                      row(ffn_norm_g[l]), w_gate, w_up, w_down, FUSED_TILE, conv_layer=l, ffn_layer=l).reshape(m_p, D_MODEL)
    c_p, kr_p, k_p, v_p = _kv_side(x.reshape(bp, t_p, D_MODEL), *kv_args(), cos32, sin32, SEQ_TILE,
                                   heads=(w_k_pad, w_v, place), t_pad=t_pad)
    for j in range(n_b):
        q = _q_proj(x.reshape(bp, t_p, D_MODEL), *q_args(j), q_tab, SEQ_TILE)
        o = _attn_prompt(q, k_p, v_p).reshape(m_p, N_HEADS * V_HEAD)
        if n_a + j < depth - 1:
            x = ffn(x, n_a + j, ROW_TILE, attn=(o, w_o, j))
        else:
            x = ffn(x, n_a + j, OUT_TILE, attn=(o, w_o, j), window=(bp, t_p, N_META))
    y_prompt = x.reshape(bp, seq, D_MODEL)

    pos_s = jnp.full((bs,), past_len, jnp.int32)
    cos32s, sin32s, q_tab_s = _rope_tables(pos_s)
    x = x_sample.reshape(bs, D_MODEL)
    conv_s = []
    state_t = jnp.transpose(state_conv, (0, 2, 1, 3))
    krope_t = jnp.swapaxes(cache_krope, 1, 2)
    for l in range(n_a):
        glu = _pw1_glu(x, row(a_norm_g[l]), pw1_w, row(a_pw1_b[l]), bs, tiles=False, layer=l)
        conv_s.append(jnp.concatenate([state_conv[l][:, 1:], glu[:, None, :]], axis=1))
        x = _conv_step(state_t[l], glu, x, a_dw_w[l], row(a_dw_b[l]),
                       row(a_ln_g[l]), row(a_ln_b[l]), pw2_w[l], row(a_pw2_b[l]))
        x = ffn(x, l, bs)
    c_s, kr_s = _kv_side(x[None], *kv_args(), cos32s, sin32s, bs)
    c_s = c_s.reshape(bs, 1, KV_LORA)
    kr_s = kr_s.reshape(bs, 1, QK_ROPE)
    for j in range(n_b):
        q = _q_proj(x[None], *q_args(j), q_tab_s, bs)[0]
        q_lat, q_rope = _q_latent(q, w_uk_t)
        o_lat = _decode_attn(page_table, jnp.transpose(q_lat, (1, 0, 2)), jnp.transpose(q_rope, (1, 0, 2)),
                             c_s, kr_s, cache_latent, krope_t)
        o = _o_latent_proj(jnp.transpose(o_lat, (1, 0, 2)), w_uv_pairs)
        x = ffn(x, n_a + j, bs, attn=(o, w_o, j))
    y_sample = x.reshape(bs, 1, D_MODEL)

    return (y_prompt, y_sample, c_p, kr_p, jnp.stack(conv_p), c_s, kr_s, jnp.stack(conv_s))
```

```python
import functools
import math

import jax
import jax.numpy as jnp
from jax import lax
from jax.experimental import pallas as pl
from jax.experimental.pallas import tpu as pltpu

D_MODEL = 1024
N_HEADS = 16
QK_NOPE = 64
QK_ROPE = 32
V_HEAD = 64
KV_LORA = 256
Q_LORA = 384
CONV_W = 31
N_META = 16
PAGE_SIZE = 128
ROPE_BASE = 10000.0
EPS = 1e-6
SCALE = 1.0 / math.sqrt(QK_NOPE + QK_ROPE)
Q_SCALE = SCALE * math.log2(math.e)

LANES = 128
HEAD_PAD = LANES
HP = N_HEADS * HEAD_PAD

BF16 = jnp.bfloat16
F32 = jnp.float32

VMEM_LIMIT = 56 * 1024 * 1024

ROW_TILE = 384
SEQ_TILE = 688
OUT_TILE = 512
Q_TILE = 256
KV_CHUNK = 256
FFN_CHUNK = 256
PW1_TILE = 1376
PW1_PARTS = 2
PW1_CHUNK = 256
FUSED_TILE = 344
FUSED_GROUP = 2
HIST = 32
DEC_PAGES = 32
DEC_SLOTS = 4


def _params(*sem):
    return pltpu.CompilerParams(dimension_semantics=sem, vmem_limit_bytes=VMEM_LIMIT)


def _const_spec(shape, layer=None):
    n = len(shape)
    if layer is None:
        return pl.BlockSpec(shape, lambda *_: (0,) * n, pipeline_mode=pl.Buffered(1))
    return pl.BlockSpec((None, *shape), lambda *_: (layer,) + (0,) * n, pipeline_mode=pl.Buffered(1))


def _rms(x, g):
    return x * lax.rsqrt(jnp.mean(x * x, axis=-1, keepdims=True) + EPS) * g


def _sigmoid(x):
    return 1.0 / (1.0 + jnp.exp(-x))


def _dot(a, b):
    return jnp.dot(a, b, preferred_element_type=F32)


def _dot_t(a, b):
    return lax.dot_general(a, b, (((1,), (1,)), ((), ())), preferred_element_type=F32)


def _swap8(cols):
    cols = list(cols)
    sub = lax.broadcasted_iota(jnp.int32, cols[0].shape, 1)
    for d in (4, 2, 1):
        keep = (sub & d) == 0
        for j in range(8):
            if j & d:
                continue
            a, b = cols[j], cols[j + d]
            cols[j] = jnp.where(keep, a, pltpu.roll(b, d, axis=1))
            cols[j + d] = jnp.where(keep, pltpu.roll(a, 8 - d, axis=1), b)
    return cols


def _pw1_kernel(x_ref, g_ref, w_ref, b_ref, o_ref, *, tiles, parts):
    rows = x_ref.shape[0] // parts
    for part in range(parts):
        r0 = part * rows
        h = _rms(x_ref[r0:r0 + rows, :], g_ref[...]).astype(BF16)
        glu = []
        for c0 in range(0, D_MODEL, PW1_CHUNK):
            val = _dot(h, w_ref[:, c0:c0 + PW1_CHUNK]) + b_ref[:, c0:c0 + PW1_CHUNK]
            gate = (_dot(h, w_ref[:, D_MODEL + c0:D_MODEL + c0 + PW1_CHUNK])
                    + b_ref[:, D_MODEL + c0:D_MODEL + c0 + PW1_CHUNK])
            glu.append(val * _sigmoid(gate))
        if not tiles:
            o_ref[r0:r0 + rows, :] = jnp.concatenate(glu, axis=1)
            continue
        per = PW1_CHUNK // LANES
        cols = _swap8([glu[j // per][:, LANES * (j % per):LANES * (j % per + 1)].reshape(rows // 8, 8, LANES)
                       for j in range(8)])
        for t in range(8):
            o_ref[r0 // 8:(r0 + rows) // 8, t] = cols[t]


def _pw1_glu(x, g, w, b, tm, tiles, layer):
    m = x.shape[0]
    if tiles:
        out_spec = pl.BlockSpec((tm // 8, 8, 8, LANES), lambda i: (i, 0, 0, 0))
        out_shape = jax.ShapeDtypeStruct((m // 8, 8, 8, LANES), F32)
    else:
        out_spec = pl.BlockSpec((tm, D_MODEL), lambda i: (i, 0))
        out_shape = jax.ShapeDtypeStruct((m, D_MODEL), F32)
    return pl.pallas_call(
        functools.partial(_pw1_kernel, tiles=tiles, parts=PW1_PARTS if tm % (8 * PW1_PARTS) == 0 and tiles else 1),
        grid=(m // tm,),
        in_specs=[
            pl.BlockSpec((tm, D_MODEL), lambda i: (i, 0)),
            _const_spec((1, D_MODEL)),
            _const_spec((D_MODEL, 2 * D_MODEL), layer),
            _const_spec((1, 2 * D_MODEL)),
        ],
        out_specs=out_spec,
        out_shape=out_shape,
        compiler_params=_params("parallel"),
        name="pw1_glu",
    )(x, g, w, b)


def _ln_swish_pw2(y, x, lng_ref, lnb_ref, w2_ref, b2_ref):
    mu = jnp.mean(y, axis=-1, keepdims=True)
    yc = y - mu
    yn = yc * lax.rsqrt(jnp.mean(yc * yc, axis=-1, keepdims=True) + EPS)
    yn = yn * lng_ref[...] + lnb_ref[...]
    z = (yn * _sigmoid(yn)).astype(BF16)
    return x + _dot(z, w2_ref[...]) + b2_ref[...]


def _conv_step_kernel(st_ref, g_ref, x_ref, dw_ref, dwb_ref, lng_ref, lnb_ref, w2_ref, b2_ref, o_ref):
    g = g_ref[...]
    y = g * dw_ref[CONV_W - 1:CONV_W, :] + dwb_ref[...]
    for k in range(CONV_W - 1):
        y = y + st_ref[k] * dw_ref[k:k + 1, :]
    o_ref[...] = _ln_swish_pw2(y, x_ref[...], lng_ref, lnb_ref, w2_ref, b2_ref)


def _conv_step(state, glu, x, dw, dwb, lng, lnb, w2, b2):
    m = x.shape[0]
    return pl.pallas_call(
        _conv_step_kernel,
        out_shape=jax.ShapeDtypeStruct((m, D_MODEL), F32),
        compiler_params=pltpu.CompilerParams(vmem_limit_bytes=VMEM_LIMIT),
        name="conv_step",
    )(state, glu, x, dw, dwb, lng, lnb, w2, b2)


def _ffn_kernel(*refs, pre, final, d_ff):
    it = iter(refs)
    x_ref = next(it)
    if pre:
        o_ref, wo_ref = next(it), next(it)
    g_ref, wg_ref, wu_ref, wd_ref = next(it), next(it), next(it), next(it)
    if final:
        gf_ref = next(it)
    out_ref, a_scr = next(it), next(it)

    x = x_ref[...]
    if pre:
        x = x + _dot(o_ref[...], wo_ref[...])
    h = _rms(x, g_ref[...]).astype(BF16)
    for c in range(d_ff // FFN_CHUNK):
        sl = slice(c * FFN_CHUNK, (c + 1) * FFN_CHUNK)
        gate = _dot(h, wg_ref[:, sl])
        up = _dot(h, wu_ref[:, sl])
        a_scr[:, sl] = (gate * _sigmoid(gate) * up).astype(BF16)
    y = x + _dot(a_scr[...], wd_ref[...])
    if final:
        y = _rms(y, gf_ref[...])
    out_ref[...] = y


def _ffn(x, g, wg, wu, wd, layer, tm, attn=None, final_g=None, window=None):
    m = x.shape[0]
    d_ff = wg.shape[2]
    if window is None:
        grid = (m // tm,)
        m_out = m
        rows_in = lambda width: pl.BlockSpec((tm, width), lambda i: (i, 0))
        rows_out = pl.BlockSpec((tm, D_MODEL), lambda i: (i, 0))
    else:
        n_seq, t_in, t_skip = window
        per_seq = (t_in - t_skip) // tm
        assert per_seq * tm == t_in - t_skip and t_in % 8 == 0 and t_skip % 8 == 0 and tm % 8 == 0
        grid = (n_seq, per_seq)
        m_out = n_seq * per_seq * tm
        first_row = lambda b, i: pl.multiple_of(b * t_in + t_skip + i * tm, 8)
        rows_in = lambda width: pl.BlockSpec((pl.Element(tm), pl.Element(width)), lambda b, i: (first_row(b, i), 0))
        rows_out = pl.BlockSpec((tm, D_MODEL), lambda b, i: (b * per_seq + i, 0))
    args = [x]
    specs = [rows_in(D_MODEL)]
    if attn is not None:
        o, wo, j = attn
        args += [o, wo]
        specs += [rows_in(o.shape[1]), _const_spec(wo.shape[1:], j)]
    args += [g, wg, wu, wd]
    specs += [_const_spec((1, D_MODEL)), _const_spec((D_MODEL, d_ff), layer),
              _const_spec((D_MODEL, d_ff), layer), _const_spec((d_ff, D_MODEL), layer)]
    if final_g is not None:
        args.append(final_g)
        specs.append(_const_spec((1, D_MODEL)))
    return pl.pallas_call(
        functools.partial(_ffn_kernel, pre=attn is not None, final=final_g is not None, d_ff=d_ff),
        grid=grid,
        in_specs=specs,
        out_specs=rows_out,
        out_shape=jax.ShapeDtypeStruct((m_out, D_MODEL), F32),
        scratch_shapes=[pltpu.VMEM((tm, d_ff), BF16)],
        compiler_params=_params(*(["parallel"] * len(grid))),
        name="ffn",
    )(*args)


def _conv_ffn_kernel(g_ref, x_ref, dw_ref, dwb_ref, lng_ref, lnb_ref, w2_ref, b2_ref,
                     fg_ref, wg_ref, wu_ref, wd_ref, o_ref, buf, y_scr, mid_scr, a_scr,
                     *, tt, tiles_per_seq, n_tiles, d_ff):
    s = pl.program_id(0)
    tile = jnp.minimum(s, n_tiles - 1)
    first_of_seq = (tile % tiles_per_seq) == 0
    prev = (s + 1) % 2

    @pl.when(s == 0)
    def _():
        mid_scr[...] = jnp.zeros_like(mid_scr)
        buf[tt:] = jnp.zeros((buf.shape[0] - tt, 8, LANES), F32)

    h = _rms(mid_scr[prev], fg_ref[...]).astype(BF16)

    buf[0:HIST] = jnp.where(first_of_seq, 0.0, buf[tt:tt + HIST])
    buf[HIST:HIST + tt] = g_ref[...]
    bias = jnp.broadcast_to(dwb_ref[...], (8, 8, LANES))
    n_ffn = d_ff // FFN_CHUNK
    per_chunk = pl.cdiv(tt // 8, n_ffn)

    def conv_group(grp):
        acc = bias
        for o in range(HIST - (CONV_W - 1), HIST + 1):
            acc = acc + buf[pl.ds(grp * 8 + o, 8)] * dw_ref[o]
        y_scr[grp] = acc

    def chunk(c):
        lo = c * FFN_CHUNK if isinstance(c, int) else pl.multiple_of(c * FFN_CHUNK, FFN_CHUNK)
        gate = _dot(h, wg_ref[:, pl.ds(lo, FFN_CHUNK)])
        up = _dot(h, wu_ref[:, pl.ds(lo, FFN_CHUNK)])
        a_scr[:, pl.ds(lo, FFN_CHUNK)] = (gate * _sigmoid(gate) * up).astype(BF16)
        for i in range(per_chunk):
            grp = c * per_chunk + i
            if not (isinstance(grp, int) and grp >= tt // 8):
                conv_group(grp)

    def chunk_group(i, carry):
        for k in range(FUSED_GROUP):
            chunk(FUSED_GROUP * i + k)
        return carry

    lax.fori_loop(0, n_ffn // FUSED_GROUP, chunk_group, 0)
    for c in range(n_ffn // FUSED_GROUP * FUSED_GROUP, n_ffn):
        chunk(c)

    cols = _swap8([y_scr[0:tt // 8, t] for t in range(8)])
    y = jnp.concatenate([c.reshape(tt, LANES) for c in cols], axis=1)
    mid_scr[s % 2] = _ln_swish_pw2(y, x_ref[...], lng_ref, lnb_ref, w2_ref, b2_ref)
    o_ref[...] = mid_scr[prev] + _dot(a_scr[...], wd_ref[...])


def _conv_ffn(glu, x, dw_tiles, dwb_tile, lng, lnb, w2, b2, fg, wg, wu, wd, tt, conv_layer, ffn_layer):
    b, t, _ = x.shape
    d_ff = wg.shape[2]
    assert t % tt == 0 and tt % 8 == 0 and d_ff % FFN_CHUNK == 0
    n_groups = tt // 8
    n_ffn = d_ff // FFN_CHUNK
    assert n_ffn // FUSED_GROUP * FUSED_GROUP * pl.cdiv(n_groups, n_ffn) <= n_groups
    tps = t // tt
    n_tiles = b * tps
    cur = lambda s: jnp.minimum(s, n_tiles - 1)
    rows_in = pl.BlockSpec((None, tt, D_MODEL), lambda s: (cur(s) // tps, cur(s) % tps, 0))
    rows_out = pl.BlockSpec((None, tt, D_MODEL),
                            lambda s: (jnp.maximum(s - 1, 0) // tps, jnp.maximum(s - 1, 0) % tps, 0))
    return pl.pallas_call(
        functools.partial(_conv_ffn_kernel, tt=tt, tiles_per_seq=tps, n_tiles=n_tiles, d_ff=d_ff),
        grid=(n_tiles + 1,),
        in_specs=[
            pl.BlockSpec((None, tt, 8, LANES), lambda s: (cur(s) // tps, cur(s) % tps, 0, 0)),
            rows_in,
            _const_spec((HIST + 8, 8, LANES)),
            _const_spec((8, LANES)),
            _const_spec((1, D_MODEL)),
            _const_spec((1, D_MODEL)),
            _const_spec((D_MODEL, D_MODEL), conv_layer),
            _const_spec((1, D_MODEL)),
            _const_spec((1, D_MODEL)),
            _const_spec((D_MODEL, d_ff), ffn_layer),
            _const_spec((D_MODEL, d_ff), ffn_layer),
            _const_spec((d_ff, D_MODEL), ffn_layer),
        ],
        out_specs=rows_out,
        out_shape=jax.ShapeDtypeStruct((b, t, D_MODEL), F32),
        scratch_shapes=[
            pltpu.VMEM((HIST + 8 * n_groups, 8, LANES), F32),
            pltpu.VMEM((n_groups, 8, 8, LANES), F32),
            pltpu.VMEM((2, tt, D_MODEL), F32),
            pltpu.VMEM((tt, d_ff), BF16),
        ],
        compiler_params=_params("arbitrary"),
        name="conv_ffn",
    )(glu, x, dw_tiles, dwb_tile, lng, lnb, w2, b2, fg, wg, wu, wd)


def _kv_kernel(*refs, heads, n_real):
    if heads:
        (x_ref, g_ref, w_ref, gl_ref, cos_ref, sin_ref, wk_ref, wv_ref, e_ref,
         c_ref, kr_ref, k_ref, v_ref) = refs
    else:
        x_ref, g_ref, w_ref, gl_ref, cos_ref, sin_ref, c_ref, kr_ref = refs

    def real():
        h = _rms(x_ref[...], g_ref[...]).astype(BF16)
        ckr = _dot(h, w_ref[...])
        c = _rms(ckr[:, :KV_LORA], gl_ref[...])
        kr = (ckr[:, KV_LORA:KV_LORA + QK_ROPE] * cos_ref[...]
              + ckr[:, KV_LORA + QK_ROPE:] * sin_ref[...])
        c_ref[...] = c
        kr_ref[...] = kr
        if heads:
            cb = c.astype(BF16)
            k_nope = _dot(cb, wk_ref[...])
            k_rope = _dot(kr.astype(BF16), e_ref[...])
            for hd in range(N_HEADS):
                sl = slice(hd * HEAD_PAD, (hd + 1) * HEAD_PAD)
                k_ref[:, sl] = (k_nope[:, sl] + k_rope).astype(BF16)
            v_ref[...] = _dot(cb, wv_ref[...]).astype(BF16)

    if not heads:
        real()
        return

    t = pl.program_id(1)
    pl.when(t < n_real)(real)

    @pl.when(t >= n_real)
    def _():
        k_ref[...] = jnp.zeros_like(k_ref)
        v_ref[...] = jnp.zeros_like(v_ref)


def _kv_side(x, g, w, gl, cos, sin, tt, heads=None, t_pad=None):
    b, t, _ = x.shape
    n_real = t // tt
    n_steps = n_real if heads is None else pl.cdiv(t_pad, tt)
    clamp = lambda bi, ti: (bi, jnp.minimum(ti, n_real - 1), 0)
    tab = lambda bi, ti: (jnp.minimum(ti, n_real - 1), 0)
    args = [x, g, w, gl, cos, sin]
    specs = [
        pl.BlockSpec((None, tt, D_MODEL), clamp),
        _const_spec((1, D_MODEL)),
        _const_spec((D_MODEL, KV_LORA + 2 * QK_ROPE)),
        _const_spec((1, KV_LORA)),
        pl.BlockSpec((tt, QK_ROPE), tab),
        pl.BlockSpec((tt, QK_ROPE), tab),
    ]
    out_specs = [pl.BlockSpec((None, tt, KV_LORA), clamp), pl.BlockSpec((None, tt, QK_ROPE), clamp)]
    out_shape = [jax.ShapeDtypeStruct((b, t, KV_LORA), F32), jax.ShapeDtypeStruct((b, t, QK_ROPE), F32)]
    if heads is not None:
        args += list(heads)
        hv = N_HEADS * V_HEAD
        specs += [_const_spec((KV_LORA, HP)), _const_spec((KV_LORA, hv)), _const_spec((QK_ROPE, HEAD_PAD))]
        full = lambda bi, ti: (bi, ti, 0)
        out_specs += [pl.BlockSpec((None, tt, HP), full), pl.BlockSpec((None, tt, hv), full)]
        out_shape += [jax.ShapeDtypeStruct((b, t_pad, HP), BF16), jax.ShapeDtypeStruct((b, t_pad, hv), BF16)]
    return pl.pallas_call(
        functools.partial(_kv_kernel, heads=heads is not None, n_real=n_real),
        grid=(b, n_steps),
        in_specs=specs,
        out_specs=out_specs,
        out_shape=out_shape,
        compiler_params=_params("parallel", "arbitrary"),
        name="kv_side",
    )(*args)


def _q_kernel(x_ref, g_ref, wdq_ref, gq_ref, wq_ref, tab_ref, q_ref):
    h = _rms(x_ref[...], g_ref[...]).astype(BF16)
    cq = _rms(_dot(h, wdq_ref[...]), gq_ref[...]).astype(BF16)
    q = _dot(cq, wq_ref[...])
    tab = tab_ref[...]
    for hd in range(N_HEADS):
        sl = slice(hd * HEAD_PAD, (hd + 1) * HEAD_PAD)
        q_ref[:, sl] = (q[:, sl] * tab).astype(BF16)


def _q_proj(x, g, wdq, gq, wq, layer, tab, tt):
    b, t, _ = x.shape
    full = lambda bi, ti: (bi, ti, 0)
    return pl.pallas_call(
        _q_kernel,
        grid=(b, t // tt),
        in_specs=[
            pl.BlockSpec((None, tt, D_MODEL), full),
            _const_spec((1, D_MODEL)),
            _const_spec((D_MODEL, Q_LORA), layer),
            _const_spec((1, Q_LORA)),
            _const_spec((Q_LORA, HP), layer),
            pl.BlockSpec((tt, HEAD_PAD), lambda bi, ti: (ti, 0)),
        ],
        out_specs=pl.BlockSpec((None, tt, HP), full),
        out_shape=jax.ShapeDtypeStruct((b, t, HP), BF16),
        compiler_params=_params("parallel", "parallel"),
        name="q_proj",
    )(x, g, wdq, gq, wq, tab)


def _query_tiles(t):
    tiles = [(q0, min(Q_TILE, t - q0)) for q0 in range(0, t, Q_TILE)]
    if len(tiles) > 1 and tiles[-1][1] < Q_TILE // 2:
        (q0, rows), (_, extra) = tiles[-2:]
        tiles[-2:] = [(q0, rows + extra)]
    return tiles


def _attn_kernel(q_ref, k_ref, v_ref, o_ref, s_scr, p_scr, *, t):
    half = KV_CHUNK // 2
    work = [(q0, rows, e) for q0, rows in _query_tiles(t) for e in range(2)]

    def chunks(q0, rows):
        return range(0, pl.cdiv(q0 + rows, KV_CHUNK) * KV_CHUNK, KV_CHUNK)

    def scores(w, q0, rows, e):
        lanes = slice(e * HEAD_PAD, (e + 1) * HEAD_PAD)
        s_buf = s_scr.at[w % 2]
        q = q_ref[q0:q0 + rows, lanes]
        mx = None
        for ks in chunks(q0, rows):
            s = _dot_t(q, k_ref[ks:ks + KV_CHUNK, lanes])
            if ks + KV_CHUNK - 1 > q0:
                qpos = q0 + lax.broadcasted_iota(jnp.int32, s.shape, 0)
                kpos = ks + lax.broadcasted_iota(jnp.int32, s.shape, 1)
                s = jnp.where(kpos <= qpos, s, -jnp.inf)
            s_buf[0:rows, ks:ks + KV_CHUNK] = s
            cm = jnp.maximum(s[:, :half], s[:, half:])
            mx = cm if mx is None else jnp.maximum(mx, cm)
        return jnp.max(mx, axis=1, keepdims=True)

    def probs(w, q0, rows, e, m):
        s_buf = s_scr.at[w % 2]
        p_buf = p_scr.at[w % 2]
        sm = None
        for ks in chunks(q0, rows):
            p = jnp.exp2(s_buf[0:rows, ks:ks + KV_CHUNK] - m)
            p_buf[0:rows, ks:ks + KV_CHUNK] = p.astype(BF16)
            ps = p[:, :half] + p[:, half:]
            sm = ps if sm is None else sm + ps
        return jnp.sum(sm, axis=1, keepdims=True)

    def values(w, q0, rows, e, l):
        n_keys = chunks(q0, rows).stop
        return _dot(p_scr[w % 2, 0:rows, 0:n_keys], v_ref[0:n_keys, :]) / l

    m_of, l_of, held = {}, {}, None
    for step in range(len(work) + 2):
        if step < len(work):
            m_of[step] = scores(step, *work[step])
        if 0 <= step - 1 < len(work):
            l_of[step - 1] = probs(step - 1, *work[step - 1], m_of.pop(step - 1))
        if 0 <= step - 2 < len(work):
            q0, rows, e = work[step - 2]
            o = values(step - 2, q0, rows, e, l_of.pop(step - 2))
            if e == 0:
                held = o
            else:
                lane = lax.broadcasted_iota(jnp.int32, o.shape, 1)
                o_ref[q0:q0 + rows, :] = jnp.where(lane < V_HEAD, held, o).astype(BF16)


def _attn_prompt(q, k, v):
    b, t, _ = q.shape
    t_pad = k.shape[1]
    pair = 2 * HEAD_PAD
    tile_rows = max(rows for _, rows in _query_tiles(t))
    return pl.pallas_call(
        functools.partial(_attn_kernel, t=t),
        grid=(b, N_HEADS // 2),
        in_specs=[pl.BlockSpec((None, t, pair), lambda bi, hi: (bi, 0, hi)),
                  pl.BlockSpec((None, t_pad, pair), lambda bi, hi: (bi, 0, hi)),
                  pl.BlockSpec((None, t_pad, 2 * V_HEAD), lambda bi, hi: (bi, 0, hi))],
        out_specs=pl.BlockSpec((None, t, 2 * V_HEAD), lambda bi, hi: (bi, 0, hi)),
        out_shape=jax.ShapeDtypeStruct((b, t, N_HEADS * V_HEAD), BF16),
        scratch_shapes=[pltpu.VMEM((2, tile_rows, t_pad), F32), pltpu.VMEM((2, tile_rows, t_pad), BF16)],
        compiler_params=_params("parallel", "parallel"),
        name="attn_prompt",
    )(q, k, v)


def _qlat_kernel(q_ref, w_ref, o_ref, r_ref):
    q = q_ref[...]
    o_ref[...] = _dot(q, w_ref[...]).astype(BF16)
    rot = q[:, QK_NOPE:].astype(F32)
    r_ref[...] = (rot[:, :QK_ROPE] + rot[:, QK_ROPE:]).astype(BF16)


def _q_latent(q, w_uk_t):
    m = q.shape[0]
    return pl.pallas_call(
        _qlat_kernel,
        grid=(N_HEADS,),
        in_specs=[pl.BlockSpec((m, HEAD_PAD), lambda h: (0, h)),
                  pl.BlockSpec((None, HEAD_PAD, KV_LORA), lambda h: (h, 0, 0))],
        out_specs=[pl.BlockSpec((None, m, KV_LORA), lambda h: (h, 0, 0)),
                   pl.BlockSpec((None, m, QK_ROPE), lambda h: (h, 0, 0))],
        out_shape=[jax.ShapeDtypeStruct((N_HEADS, m, KV_LORA), BF16),
                   jax.ShapeDtypeStruct((N_HEADS, m, QK_ROPE), BF16)],
        compiler_params=_params("parallel"),
        name="q_latent",
    )(q, w_uk_t)


def _decode_kernel(pt_ref, ql_ref, qr_ref, cn_ref, krn_ref, lat_hbm, krt_hbm, o_ref,
                   lat_buf, kr_buf, lat_sem, kr_sem, m_scr, l_scr, acc_scr, *, groups):
    n_seq = ql_ref.shape[0]
    n_iter = n_seq * groups // 2
    lanes = lambda a, n: jnp.concatenate([a] * n, axis=1)

    def copies(t, slot):
        b = t // groups
        first_page = (t % groups) * DEC_PAGES
        out = []
        for j in range(DEC_PAGES):
            page = pt_ref[b, first_page + j]
            out.append(pltpu.make_async_copy(
                lat_hbm.at[page], lat_buf.at[slot, pl.ds(j * PAGE_SIZE, PAGE_SIZE)], lat_sem.at[slot]))
            out.append(pltpu.make_async_copy(krt_hbm.at[page], kr_buf.at[slot, j], kr_sem.at[slot]))
        return out

    def start(t, slot):
        for cp in copies(t, slot):
            cp.start()

    def wait(t, slot):
        for cp in copies(t, slot):
            cp.wait()

    def group(slot, ql, qr):
        c = lat_buf[slot].astype(BF16)
        kr_t = jnp.concatenate([kr_buf[slot, j] for j in range(DEC_PAGES)], axis=1).astype(BF16)
        s = _dot_t(ql, c) + _dot(qr, kr_t)
        m = jnp.max(s, axis=1, keepdims=True)
        p = jnp.exp2(s - m)
        return m, jnp.sum(p, axis=1, keepdims=True), _dot(p.astype(BF16), c)

    start(0, 0)
    start(1, 1)

    def body(i, carry):
        base = (i % 2) * 2
        b = (2 * i) // groups
        ql = ql_ref[b]
        qr = qr_ref[b]

        @pl.when(i + 1 < n_iter)
        def _():
            start(2 * i + 2, 2 - base)
            start(2 * i + 3, 3 - base)

        @pl.when((2 * i) % groups == 0)
        def _():
            cn = cn_ref[b].astype(BF16).astype(F32)
            krn = krn_ref[b].astype(BF16).astype(F32)
            s_new = (jnp.sum(ql.astype(F32) * cn, axis=1, keepdims=True)
                     + jnp.sum(qr.astype(F32) * krn, axis=1, keepdims=True))
            m_scr[...] = jnp.broadcast_to(s_new, m_scr.shape)
            l_scr[...] = jnp.ones_like(l_scr)
            acc_scr[...] = jnp.broadcast_to(cn, acc_scr.shape)

        wait(2 * i, base)
        wait(2 * i + 1, base + 1)
        m0, l0, o0 = group(base, ql, qr)
        m1, l1, o1 = group(base + 1, ql, qr)

        m_prev = m_scr[...]
        m_new = jnp.maximum(m_prev, jnp.maximum(m0, m1))
        a_prev = jnp.exp2(m_prev - m_new)
        a0 = jnp.exp2(m0 - m_new)
        a1 = jnp.exp2(m1 - m_new)
        l_scr[...] = a_prev * l_scr[...] + a0 * l0 + a1 * l1
        acc_scr[...] = lanes(a_prev, 2) * acc_scr[...] + lanes(a0, 2) * o0 + lanes(a1, 2) * o1
        m_scr[...] = m_new

        @pl.when((2 * i + 2) % groups == 0)
        def _():
            o_ref[b] = acc_scr[...] / lanes(l_scr[...], 2)

        return carry

    lax.fori_loop(0, n_iter, body, 0)


def _decode_attn(page_table, q_lat, q_rope, c_new, kr_new, cache_latent, cache_krope_t):
    b, n_pages = page_table.shape
    groups = n_pages // DEC_PAGES
    assert n_pages % DEC_PAGES == 0 and groups % 2 == 0
    vmem = pl.BlockSpec(memory_space=pltpu.VMEM)
    return pl.pallas_call(
        functools.partial(_decode_kernel, groups=groups),
        in_specs=[pl.BlockSpec(memory_space=pltpu.SMEM), vmem, vmem, vmem, vmem,
                  pl.BlockSpec(memory_space=pl.ANY), pl.BlockSpec(memory_space=pl.ANY)],
        out_specs=vmem,
        out_shape=jax.ShapeDtypeStruct((b, N_HEADS, KV_LORA), F32),
        scratch_shapes=[
            pltpu.VMEM((DEC_SLOTS, DEC_PAGES * PAGE_SIZE, KV_LORA), F32),
            pltpu.VMEM((DEC_SLOTS, DEC_PAGES, QK_ROPE, PAGE_SIZE), F32),
            pltpu.SemaphoreType.DMA((DEC_SLOTS,)),
            pltpu.SemaphoreType.DMA((DEC_SLOTS,)),
            pltpu.VMEM((N_HEADS, HEAD_PAD), F32),
            pltpu.VMEM((N_HEADS, HEAD_PAD), F32),
            pltpu.VMEM((N_HEADS, KV_LORA), F32),
        ],
        compiler_params=pltpu.CompilerParams(vmem_limit_bytes=VMEM_LIMIT),
        name="decode_attn",
    )(page_table, q_lat, q_rope, c_new, kr_new, cache_latent, cache_krope_t)


def _olat_kernel(o_ref, w_ref, out_ref):
    out_ref[...] = (_dot(o_ref[0].astype(BF16), w_ref[0]) + _dot(o_ref[1].astype(BF16), w_ref[1])).astype(BF16)


def _o_latent_proj(o_lat, w_uv_pairs):
    m = o_lat.shape[1]
    return pl.pallas_call(
        _olat_kernel,
        grid=(N_HEADS // 2,),
        in_specs=[pl.BlockSpec((2, m, KV_LORA), lambda h: (h, 0, 0)),
                  pl.BlockSpec((2, KV_LORA, LANES), lambda h: (h, 0, 0))],
        out_specs=pl.BlockSpec((m, LANES), lambda h: (0, h)),
        out_shape=jax.ShapeDtypeStruct((m, N_HEADS * V_HEAD), BF16),
        compiler_params=_params("parallel"),
        name="o_latent_proj",
    )(o_lat, w_uv_pairs)


def _rope_tables(pos):
    half = QK_ROPE // 2
    inv = jnp.power(ROPE_BASE, -jnp.arange(half, dtype=F32) / half)
    ang = pos.astype(F32)[:, None] * inv[None, :]
    cos = jnp.cos(ang)
    sin = jnp.sin(ang)
    cos32 = jnp.concatenate([cos, cos], axis=1)
    sin32 = jnp.concatenate([sin, sin], axis=1)
    q_tab = Q_SCALE * jnp.concatenate([jnp.ones((pos.shape[0], QK_NOPE), F32), cos32, sin32], axis=1)
    return cos32, sin32, q_tab


def _rot_half_cols(w):
    half = QK_ROPE // 2
    return jnp.concatenate([-w[..., half:], w[..., :half]], axis=-1)


def _pad_heads(w, width):
    r = w.shape[0]
    return jnp.pad(w, ((0, 0), (0, 0), (0, HEAD_PAD - width))).reshape(r, HP)


def kernel(x_prompt, x_sample, cache_latent, cache_krope, state_conv, page_table, meta_tokens, a_norm_g, a_pw1_w, a_pw1_b, a_dw_w, a_dw_b, a_ln_g, a_ln_b, a_pw2_w, a_pw2_b, ffn_norm_g, ffn_w_gate, ffn_w_up, ffn_w_down, kv_norm_g, w_dkv, kv_latent_norm_g, w_uk, w_uv, b_norm_g, b_w_dq, b_q_norm_g, b_w_uq, b_w_o, final_norm_g):
    n_a = a_norm_g.shape[0]
    depth = ffn_norm_g.shape[0]
    n_b = depth - n_a
    bp, seq, _ = x_prompt.shape
    t_p = seq + N_META
    bs = x_sample.shape[0]
    past_len = page_table.shape[1] * PAGE_SIZE

    row = lambda v: v.reshape(1, -1).astype(F32)

    pw1_w = a_pw1_w.astype(BF16)
    pw2_w = a_pw2_w.astype(BF16)
    w_gate = ffn_w_gate.astype(BF16)
    w_up = ffn_w_up.astype(BF16)
    w_down = ffn_w_down.astype(BF16)
    first_tap = HIST - (CONV_W - 1)
    dw_tiles = jnp.pad(a_dw_w, ((0, 0), (first_tap, HIST + 8 - first_tap - CONV_W), (0, 0)))
    dw_tiles = dw_tiles.reshape(n_a, HIST + 8, 8, LANES)
    w_ckr = jnp.concatenate([w_dkv, _rot_half_cols(w_dkv[:, KV_LORA:])], axis=1).astype(BF16)
    w_k_pad = _pad_heads(w_uk, QK_NOPE).astype(BF16)
    w_v = w_uv.reshape(KV_LORA, N_HEADS * V_HEAD).astype(BF16)
    eye = jnp.eye(QK_ROPE, dtype=BF16)
    place = jnp.concatenate([jnp.zeros((QK_ROPE, QK_NOPE), BF16), eye, eye], axis=1)
    w_dq = b_w_dq.astype(BF16)
    w_q_pad = jnp.stack([
        _pad_heads(jnp.concatenate([b_w_uq[j], _rot_half_cols(b_w_uq[j][..., QK_NOPE:])], axis=-1), HEAD_PAD)
        for j in range(n_b)]).astype(BF16)
    w_o = b_w_o.astype(BF16)
    w_uk_t = jnp.pad(jnp.transpose(w_uk, (1, 2, 0)), ((0, 0), (0, HEAD_PAD - QK_NOPE), (0, 0))).astype(BF16)
    w_uv_h = jnp.transpose(w_uv, (1, 0, 2))
    w_uv_pairs = jnp.where((jnp.arange(N_HEADS) % 2 == 0)[:, None, None],
                           jnp.pad(w_uv_h, ((0, 0), (0, 0), (0, V_HEAD))),
                           jnp.pad(w_uv_h, ((0, 0), (0, 0), (V_HEAD, 0)))).astype(BF16)

    def ffn(x, l, tm, attn=None, window=None):
        final = row(final_norm_g) if l == depth - 1 else None
        return _ffn(x, row(ffn_norm_g[l]), w_gate, w_up, w_down, l, tm, attn=attn, final_g=final, window=window)

    def kv_args():
        return (row(kv_norm_g), w_ckr, row(kv_latent_norm_g))

    def q_args(j):
        return (row(b_norm_g[j]), w_dq, row(b_q_norm_g[j]), w_q_pad, j)

    xp = jnp.concatenate([jnp.broadcast_to(meta_tokens[None].astype(F32), (bp, N_META, D_MODEL)), x_prompt], axis=1)
    cos32, sin32, q_tab = _rope_tables(jnp.arange(t_p))
    m_p = bp * t_p
    t_pad = pl.cdiv(t_p, KV_CHUNK) * KV_CHUNK
    x = xp.reshape(m_p, D_MODEL)
    conv_p = []
    for l in range(n_a):
        glu = _pw1_glu(x, row(a_norm_g[l]), pw1_w, row(a_pw1_b[l]), PW1_TILE, tiles=True, layer=l)
        glu = glu.reshape(bp, t_p, 8, LANES)
        conv_p.append(glu[:, t_p - (CONV_W - 1):].reshape(bp, CONV_W - 1, D_MODEL))
        x = _conv_ffn(glu, x.reshape(bp, t_p, D_MODEL), dw_tiles[l], a_dw_b[l].reshape(8, LANES),
                      row(a_ln_g[l]), row(a_ln_b[l]), pw2_w, row(a_pw2_b[l]),
                      row(ffn_norm_g[l]), w_gate, w_up, w_down, FUSED_TILE, conv_layer=l, ffn_layer=l).reshape(m_p, D_MODEL)
    c_p, kr_p, k_p, v_p = _kv_side(x.reshape(bp, t_p, D_MODEL), *kv_args(), cos32, sin32, SEQ_TILE,
                                   heads=(w_k_pad, w_v, place), t_pad=t_pad)
    for j in range(n_b):
        q = _q_proj(x.reshape(bp, t_p, D_MODEL), *q_args(j), q_tab, SEQ_TILE)
        o = _attn_prompt(q, k_p, v_p).reshape(m_p, N_HEADS * V_HEAD)
        if n_a + j < depth - 1:
            x = ffn(x, n_a + j, ROW_TILE, attn=(o, w_o, j))
        else:
            x = ffn(x, n_a + j, OUT_TILE, attn=(o, w_o, j), window=(bp, t_p, N_META))
    y_prompt = x.reshape(bp, seq, D_MODEL)

    pos_s = jnp.full((bs,), past_len, jnp.int32)
    cos32s, sin32s, q_tab_s = _rope_tables(pos_s)
    x = x_sample.reshape(bs, D_MODEL)
    conv_s = []
    state_t = jnp.transpose(state_conv, (0, 2, 1, 3))
    krope_t = jnp.swapaxes(cache_krope, 1, 2)
    for l in range(n_a):
        glu = _pw1_glu(x, row(a_norm_g[l]), pw1_w, row(a_pw1_b[l]), bs, tiles=False, layer=l)
        conv_s.append(jnp.concatenate([state_conv[l][:, 1:], glu[:, None, :]], axis=1))
        x = _conv_step(state_t[l], glu, x, a_dw_w[l], row(a_dw_b[l]),
                       row(a_ln_g[l]), row(a_ln_b[l]), pw2_w[l], row(a_pw2_b[l]))
        x = ffn(x, l, bs)
    c_s, kr_s = _kv_side(x[None], *kv_args(), cos32s, sin32s, bs)
    c_s = c_s.reshape(bs, 1, KV_LORA)
    kr_s = kr_s.reshape(bs, 1, QK_ROPE)
    for j in range(n_b):
        q = _q_proj(x[None], *q_args(j), q_tab_s, bs)[0]
        q_lat, q_rope = _q_latent(q, w_uk_t)
        o_lat = _decode_attn(page_table, jnp.transpose(q_lat, (1, 0, 2)), jnp.transpose(q_rope, (1, 0, 2)),
                             c_s, kr_s, cache_latent, krope_t)
        o = _o_latent_proj(jnp.transpose(o_lat, (1, 0, 2)), w_uv_pairs)
        x = ffn(x, n_a + j, bs, attn=(o, w_o, j))
    y_sample = x.reshape(bs, 1, D_MODEL)

    return (y_prompt, y_sample, c_p, kr_p, jnp.stack(conv_p), c_s, kr_s, jnp.stack(conv_s))
```

```python
import functools
import math

import jax
import jax.numpy as jnp
from jax import lax
from jax.experimental import pallas as pl
from jax.experimental.pallas import tpu as pltpu

D_MODEL = 1024
N_HEADS = 16
QK_NOPE = 64
QK_ROPE = 32
V_HEAD = 64
KV_LORA = 256
Q_LORA = 384
CONV_W = 31
N_META = 16
PAGE_SIZE = 128
ROPE_BASE = 10000.0
EPS = 1e-6
SCALE = 1.0 / math.sqrt(QK_NOPE + QK_ROPE)
Q_SCALE = SCALE * math.log2(math.e)

LANES = 128
HEAD_PAD = LANES
HP = N_HEADS * HEAD_PAD

BF16 = jnp.bfloat16
F32 = jnp.float32

VMEM_LIMIT = 56 * 1024 * 1024

ROW_TILE = 384
SEQ_TILE = 688
OUT_TILE = 512
Q_TILE = 256
KV_CHUNK = 256
FFN_CHUNK = 256
PW1_TILE = 1376
PW1_PARTS = 2
PW1_CHUNK = 256
FUSED_TILE = 344
FUSED_GROUP = 2
HIST = 32
DEC_PAGES = 64
DEC_SLOTS = 4


def _params(*sem):
    return pltpu.CompilerParams(dimension_semantics=sem, vmem_limit_bytes=VMEM_LIMIT)


def _const_spec(shape, layer=None):
    n = len(shape)
    if layer is None:
        return pl.BlockSpec(shape, lambda *_: (0,) * n, pipeline_mode=pl.Buffered(1))
    return pl.BlockSpec((None, *shape), lambda *_: (layer,) + (0,) * n, pipeline_mode=pl.Buffered(1))


def _rms(x, g):
    return x * lax.rsqrt(jnp.mean(x * x, axis=-1, keepdims=True) + EPS) * g


def _sigmoid(x):
    return 1.0 / (1.0 + jnp.exp(-x))


def _dot(a, b):
    return jnp.dot(a, b, preferred_element_type=F32)


def _dot_t(a, b):
    return lax.dot_general(a, b, (((1,), (1,)), ((), ())), preferred_element_type=F32)


def _swap8(cols):
    cols = list(cols)
    sub = lax.broadcasted_iota(jnp.int32, cols[0].shape, 1)
    for d in (4, 2, 1):
        keep = (sub & d) == 0
        for j in range(8):
            if j & d:
                continue
            a, b = cols[j], cols[j + d]
            cols[j] = jnp.where(keep, a, pltpu.roll(b, d, axis=1))
            cols[j + d] = jnp.where(keep, pltpu.roll(a, 8 - d, axis=1), b)
    return cols


def _pw1_kernel(x_ref, g_ref, w_ref, b_ref, o_ref, *, tiles, parts):
    rows = x_ref.shape[0] // parts
    for part in range(parts):
        r0 = part * rows
        h = _rms(x_ref[r0:r0 + rows, :], g_ref[...]).astype(BF16)
        glu = []
        for c0 in range(0, D_MODEL, PW1_CHUNK):
            val = _dot(h, w_ref[:, c0:c0 + PW1_CHUNK]) + b_ref[:, c0:c0 + PW1_CHUNK]
            gate = (_dot(h, w_ref[:, D_MODEL + c0:D_MODEL + c0 + PW1_CHUNK])
                    + b_ref[:, D_MODEL + c0:D_MODEL + c0 + PW1_CHUNK])
            glu.append(val * _sigmoid(gate))
        if not tiles:
            o_ref[r0:r0 + rows, :] = jnp.concatenate(glu, axis=1)
            continue
        per = PW1_CHUNK // LANES
        cols = _swap8([glu[j // per][:, LANES * (j % per):LANES * (j % per + 1)].reshape(rows // 8, 8, LANES)
                       for j in range(8)])
        for t in range(8):
            o_ref[r0 // 8:(r0 + rows) // 8, t] = cols[t]


def _pw1_glu(x, g, w, b, tm, tiles, layer):
    m = x.shape[0]
    if tiles:
        out_spec = pl.BlockSpec((tm // 8, 8, 8, LANES), lambda i: (i, 0, 0, 0))
        out_shape = jax.ShapeDtypeStruct((m // 8, 8, 8, LANES), F32)
    else:
        out_spec = pl.BlockSpec((tm, D_MODEL), lambda i: (i, 0))
        out_shape = jax.ShapeDtypeStruct((m, D_MODEL), F32)
    return pl.pallas_call(
        functools.partial(_pw1_kernel, tiles=tiles, parts=PW1_PARTS if tm % (8 * PW1_PARTS) == 0 and tiles else 1),
        grid=(m // tm,),
        in_specs=[
            pl.BlockSpec((tm, D_MODEL), lambda i: (i, 0)),
            _const_spec((1, D_MODEL)),
            _const_spec((D_MODEL, 2 * D_MODEL), layer),
            _const_spec((1, 2 * D_MODEL)),
        ],
        out_specs=out_spec,
        out_shape=out_shape,
        compiler_params=_params("parallel"),
        name="pw1_glu",
    )(x, g, w, b)


def _ln_swish_pw2(y, x, lng_ref, lnb_ref, w2_ref, b2_ref):
    mu = jnp.mean(y, axis=-1, keepdims=True)
    yc = y - mu
    yn = yc * lax.rsqrt(jnp.mean(yc * yc, axis=-1, keepdims=True) + EPS)
    yn = yn * lng_ref[...] + lnb_ref[...]
    z = (yn * _sigmoid(yn)).astype(BF16)
    return x + _dot(z, w2_ref[...]) + b2_ref[...]


def _conv_step_kernel(st_ref, g_ref, x_ref, dw_ref, dwb_ref, lng_ref, lnb_ref, w2_ref, b2_ref, o_ref):
    g = g_ref[...]
    y = g * dw_ref[CONV_W - 1:CONV_W, :] + dwb_ref[...]
    for k in range(CONV_W - 1):
        y = y + st_ref[k] * dw_ref[k:k + 1, :]
    o_ref[...] = _ln_swish_pw2(y, x_ref[...], lng_ref, lnb_ref, w2_ref, b2_ref)


def _conv_step(state, glu, x, dw, dwb, lng, lnb, w2, b2):
    m = x.shape[0]
    return pl.pallas_call(
        _conv_step_kernel,
        out_shape=jax.ShapeDtypeStruct((m, D_MODEL), F32),
        compiler_params=pltpu.CompilerParams(vmem_limit_bytes=VMEM_LIMIT),
        name="conv_step",
    )(state, glu, x, dw, dwb, lng, lnb, w2, b2)


def _ffn_kernel(*refs, pre, final, d_ff):
    it = iter(refs)
    x_ref = next(it)
    if pre:
        o_ref, wo_ref = next(it), next(it)
    g_ref, wg_ref, wu_ref, wd_ref = next(it), next(it), next(it), next(it)
    if final:
        gf_ref = next(it)
    out_ref, a_scr = next(it), next(it)

    x = x_ref[...]
    if pre:
        x = x + _dot(o_ref[...], wo_ref[...])
    h = _rms(x, g_ref[...]).astype(BF16)
    for c in range(d_ff // FFN_CHUNK):
        sl = slice(c * FFN_CHUNK, (c + 1) * FFN_CHUNK)
        gate = _dot(h, wg_ref[:, sl])
        up = _dot(h, wu_ref[:, sl])
        a_scr[:, sl] = (gate * _sigmoid(gate) * up).astype(BF16)
    y = x + _dot(a_scr[...], wd_ref[...])
    if final:
        y = _rms(y, gf_ref[...])
    out_ref[...] = y


def _ffn(x, g, wg, wu, wd, layer, tm, attn=None, final_g=None, window=None):
    m = x.shape[0]
    d_ff = wg.shape[2]
    if window is None:
        grid = (m // tm,)
        m_out = m
        rows_in = lambda width: pl.BlockSpec((tm, width), lambda i: (i, 0))
        rows_out = pl.BlockSpec((tm, D_MODEL), lambda i: (i, 0))
    else:
        n_seq, t_in, t_skip = window
        per_seq = (t_in - t_skip) // tm
        assert per_seq * tm == t_in - t_skip and t_in % 8 == 0 and t_skip % 8 == 0 and tm % 8 == 0
        grid = (n_seq, per_seq)
        m_out = n_seq * per_seq * tm
        first_row = lambda b, i: pl.multiple_of(b * t_in + t_skip + i * tm, 8)
        rows_in = lambda width: pl.BlockSpec((pl.Element(tm), pl.Element(width)), lambda b, i: (first_row(b, i), 0))
        rows_out = pl.BlockSpec((tm, D_MODEL), lambda b, i: (b * per_seq + i, 0))
    args = [x]
    specs = [rows_in(D_MODEL)]
    if attn is not None:
        o, wo, j = attn
        args += [o, wo]
        specs += [rows_in(o.shape[1]), _const_spec(wo.shape[1:], j)]
    args += [g, wg, wu, wd]
    specs += [_const_spec((1, D_MODEL)), _const_spec((D_MODEL, d_ff), layer),
              _const_spec((D_MODEL, d_ff), layer), _const_spec((d_ff, D_MODEL), layer)]
    if final_g is not None:
        args.append(final_g)
        specs.append(_const_spec((1, D_MODEL)))
    return pl.pallas_call(
        functools.partial(_ffn_kernel, pre=attn is not None, final=final_g is not None, d_ff=d_ff),
        grid=grid,
        in_specs=specs,
        out_specs=rows_out,
        out_shape=jax.ShapeDtypeStruct((m_out, D_MODEL), F32),
        scratch_shapes=[pltpu.VMEM((tm, d_ff), BF16)],
        compiler_params=_params(*(["parallel"] * len(grid))),
        name="ffn",
    )(*args)


def _conv_ffn_kernel(g_ref, x_ref, dw_ref, dwb_ref, lng_ref, lnb_ref, w2_ref, b2_ref,
                     fg_ref, wg_ref, wu_ref, wd_ref, o_ref, buf, y_scr, mid_scr, a_scr,
                     *, tt, tiles_per_seq, n_tiles, d_ff):
    s = pl.program_id(0)
    tile = jnp.minimum(s, n_tiles - 1)
    first_of_seq = (tile % tiles_per_seq) == 0
    prev = (s + 1) % 2

    @pl.when(s == 0)
    def _():
        mid_scr[...] = jnp.zeros_like(mid_scr)
        buf[tt:] = jnp.zeros((buf.shape[0] - tt, 8, LANES), F32)

    h = _rms(mid_scr[prev], fg_ref[...]).astype(BF16)

    buf[0:HIST] = jnp.where(first_of_seq, 0.0, buf[tt:tt + HIST])
    buf[HIST:HIST + tt] = g_ref[...]
    bias = jnp.broadcast_to(dwb_ref[...], (8, 8, LANES))
    n_ffn = d_ff // FFN_CHUNK
    per_chunk = pl.cdiv(tt // 8, n_ffn)

    def conv_group(grp):
        acc = bias
        for o in range(HIST - (CONV_W - 1), HIST + 1):
            acc = acc + buf[pl.ds(grp * 8 + o, 8)] * dw_ref[o]
        y_scr[grp] = acc

    def chunk(c):
        lo = c * FFN_CHUNK if isinstance(c, int) else pl.multiple_of(c * FFN_CHUNK, FFN_CHUNK)
        gate = _dot(h, wg_ref[:, pl.ds(lo, FFN_CHUNK)])
        up = _dot(h, wu_ref[:, pl.ds(lo, FFN_CHUNK)])
        a_scr[:, pl.ds(lo, FFN_CHUNK)] = (gate * _sigmoid(gate) * up).astype(BF16)
        for i in range(per_chunk):
            grp = c * per_chunk + i
            if not (isinstance(grp, int) and grp >= tt // 8):
                conv_group(grp)

    def chunk_group(i, carry):
        for k in range(FUSED_GROUP):
            chunk(FUSED_GROUP * i + k)
        return carry

    lax.fori_loop(0, n_ffn // FUSED_GROUP, chunk_group, 0)
    for c in range(n_ffn // FUSED_GROUP * FUSED_GROUP, n_ffn):
        chunk(c)

    cols = _swap8([y_scr[0:tt // 8, t] for t in range(8)])
    y = jnp.concatenate([c.reshape(tt, LANES) for c in cols], axis=1)
    mid_scr[s % 2] = _ln_swish_pw2(y, x_ref[...], lng_ref, lnb_ref, w2_ref, b2_ref)
    o_ref[...] = mid_scr[prev] + _dot(a_scr[...], wd_ref[...])


def _conv_ffn(glu, x, dw_tiles, dwb_tile, lng, lnb, w2, b2, fg, wg, wu, wd, tt, conv_layer, ffn_layer):
    b, t, _ = x.shape
    d_ff = wg.shape[2]
    assert t % tt == 0 and tt % 8 == 0 and d_ff % FFN_CHUNK == 0
    n_groups = tt // 8
    n_ffn = d_ff // FFN_CHUNK
    assert n_ffn // FUSED_GROUP * FUSED_GROUP * pl.cdiv(n_groups, n_ffn) <= n_groups
    tps = t // tt
    n_tiles = b * tps
    cur = lambda s: jnp.minimum(s, n_tiles - 1)
    rows_in = pl.BlockSpec((None, tt, D_MODEL), lambda s: (cur(s) // tps, cur(s) % tps, 0))
    rows_out = pl.BlockSpec((None, tt, D_MODEL),
                            lambda s: (jnp.maximum(s - 1, 0) // tps, jnp.maximum(s - 1, 0) % tps, 0))
    return pl.pallas_call(
        functools.partial(_conv_ffn_kernel, tt=tt, tiles_per_seq=tps, n_tiles=n_tiles, d_ff=d_ff),
        grid=(n_tiles + 1,),
        in_specs=[
            pl.BlockSpec((None, tt, 8, LANES), lambda s: (cur(s) // tps, cur(s) % tps, 0, 0)),
            rows_in,
            _const_spec((HIST + 8, 8, LANES)),
            _const_spec((8, LANES)),
            _const_spec((1, D_MODEL)),
            _const_spec((1, D_MODEL)),
            _const_spec((D_MODEL, D_MODEL), conv_layer),
            _const_spec((1, D_MODEL)),
            _const_spec((1, D_MODEL)),
            _const_spec((D_MODEL, d_ff), ffn_layer),
            _const_spec((D_MODEL, d_ff), ffn_layer),
            _const_spec((d_ff, D_MODEL), ffn_layer),
        ],
        out_specs=rows_out,
        out_shape=jax.ShapeDtypeStruct((b, t, D_MODEL), F32),
        scratch_shapes=[
            pltpu.VMEM((HIST + 8 * n_groups, 8, LANES), F32),
            pltpu.VMEM((n_groups, 8, 8, LANES), F32),
            pltpu.VMEM((2, tt, D_MODEL), F32),
            pltpu.VMEM((tt, d_ff), BF16),
        ],
        compiler_params=_params("arbitrary"),
        name="conv_ffn",
    )(glu, x, dw_tiles, dwb_tile, lng, lnb, w2, b2, fg, wg, wu, wd)


def _kv_kernel(*refs, heads, n_real):
    if heads:
        (x_ref, g_ref, w_ref, gl_ref, cos_ref, sin_ref, wk_ref, wv_ref, e_ref,
         c_ref, kr_ref, k_ref, v_ref) = refs
    else:
        x_ref, g_ref, w_ref, gl_ref, cos_ref, sin_ref, c_ref, kr_ref = refs

    def real():
        h = _rms(x_ref[...], g_ref[...]).astype(BF16)
        ckr = _dot(h, w_ref[...])
        c = _rms(ckr[:, :KV_LORA], gl_ref[...])
        kr = (ckr[:, KV_LORA:KV_LORA + QK_ROPE] * cos_ref[...]
              + ckr[:, KV_LORA + QK_ROPE:] * sin_ref[...])
        c_ref[...] = c
        kr_ref[...] = kr
        if heads:
            cb = c.astype(BF16)
            k_nope = _dot(cb, wk_ref[...])
            k_rope = _dot(kr.astype(BF16), e_ref[...])
            for hd in range(N_HEADS):
                sl = slice(hd * HEAD_PAD, (hd + 1) * HEAD_PAD)
                k_ref[:, sl] = (k_nope[:, sl] + k_rope).astype(BF16)
            v_ref[...] = _dot(cb, wv_ref[...]).astype(BF16)

    if not heads:
        real()
        return

    t = pl.program_id(1)
    pl.when(t < n_real)(real)

    @pl.when(t >= n_real)
    def _():
        k_ref[...] = jnp.zeros_like(k_ref)
        v_ref[...] = jnp.zeros_like(v_ref)


def _kv_side(x, g, w, gl, cos, sin, tt, heads=None, t_pad=None):
    b, t, _ = x.shape
    n_real = t // tt
    n_steps = n_real if heads is None else pl.cdiv(t_pad, tt)
    clamp = lambda bi, ti: (bi, jnp.minimum(ti, n_real - 1), 0)
    tab = lambda bi, ti: (jnp.minimum(ti, n_real - 1), 0)
    args = [x, g, w, gl, cos, sin]
    specs = [
        pl.BlockSpec((None, tt, D_MODEL), clamp),
        _const_spec((1, D_MODEL)),
        _const_spec((D_MODEL, KV_LORA + 2 * QK_ROPE)),
        _const_spec((1, KV_LORA)),
        pl.BlockSpec((tt, QK_ROPE), tab),
        pl.BlockSpec((tt, QK_ROPE), tab),
    ]
    out_specs = [pl.BlockSpec((None, tt, KV_LORA), clamp), pl.BlockSpec((None, tt, QK_ROPE), clamp)]
    out_shape = [jax.ShapeDtypeStruct((b, t, KV_LORA), F32), jax.ShapeDtypeStruct((b, t, QK_ROPE), F32)]
    if heads is not None:
        args += list(heads)
        hv = N_HEADS * V_HEAD
        specs += [_const_spec((KV_LORA, HP)), _const_spec((KV_LORA, hv)), _const_spec((QK_ROPE, HEAD_PAD))]
        full = lambda bi, ti: (bi, ti, 0)
        out_specs += [pl.BlockSpec((None, tt, HP), full), pl.BlockSpec((None, tt, hv), full)]
        out_shape += [jax.ShapeDtypeStruct((b, t_pad, HP), BF16), jax.ShapeDtypeStruct((b, t_pad, hv), BF16)]
    return pl.pallas_call(
        functools.partial(_kv_kernel, heads=heads is not None, n_real=n_real),
        grid=(b, n_steps),
        in_specs=specs,
        out_specs=out_specs,
        out_shape=out_shape,
        compiler_params=_params("parallel", "arbitrary"),
        name="kv_side",
    )(*args)


def _q_kernel(x_ref, g_ref, wdq_ref, gq_ref, wq_ref, tab_ref, q_ref):
    h = _rms(x_ref[...], g_ref[...]).astype(BF16)
    cq = _rms(_dot(h, wdq_ref[...]), gq_ref[...]).astype(BF16)
    q = _dot(cq, wq_ref[...])
    tab = tab_ref[...]
    for hd in range(N_HEADS):
        sl = slice(hd * HEAD_PAD, (hd + 1) * HEAD_PAD)
        q_ref[:, sl] = (q[:, sl] * tab).astype(BF16)


def _q_proj(x, g, wdq, gq, wq, layer, tab, tt):
    b, t, _ = x.shape
    full = lambda bi, ti: (bi, ti, 0)
    return pl.pallas_call(
        _q_kernel,
        grid=(b, t // tt),
        in_specs=[
            pl.BlockSpec((None, tt, D_MODEL), full),
            _const_spec((1, D_MODEL)),
            _const_spec((D_MODEL, Q_LORA), layer),
            _const_spec((1, Q_LORA)),
            _const_spec((Q_LORA, HP), layer),
            pl.BlockSpec((tt, HEAD_PAD), lambda bi, ti: (ti, 0)),
        ],
        out_specs=pl.BlockSpec((None, tt, HP), full),
        out_shape=jax.ShapeDtypeStruct((b, t, HP), BF16),
        compiler_params=_params("parallel", "parallel"),
        name="q_proj",
    )(x, g, wdq, gq, wq, tab)


def _query_tiles(t):
    tiles = [(q0, min(Q_TILE, t - q0)) for q0 in range(0, t, Q_TILE)]
    if len(tiles) > 1 and tiles[-1][1] < Q_TILE // 2:
        (q0, rows), (_, extra) = tiles[-2:]
        tiles[-2:] = [(q0, rows + extra)]
    return tiles


def _attn_kernel(q_ref, k_ref, v_ref, o_ref, s_scr, p_scr, *, t):
    half = KV_CHUNK // 2
    work = [(q0, rows, e) for q0, rows in _query_tiles(t) for e in range(2)]

    def chunks(q0, rows):
        return range(0, pl.cdiv(q0 + rows, KV_CHUNK) * KV_CHUNK, KV_CHUNK)

    def scores(w, q0, rows, e):
        lanes = slice(e * HEAD_PAD, (e + 1) * HEAD_PAD)
        s_buf = s_scr.at[w % 2]
        q = q_ref[q0:q0 + rows, lanes]
        mx = None
        for ks in chunks(q0, rows):
            s = _dot_t(q, k_ref[ks:ks + KV_CHUNK, lanes])
            if ks + KV_CHUNK - 1 > q0:
                qpos = q0 + lax.broadcasted_iota(jnp.int32, s.shape, 0)
                kpos = ks + lax.broadcasted_iota(jnp.int32, s.shape, 1)
                s = jnp.where(kpos <= qpos, s, -jnp.inf)
            s_buf[0:rows, ks:ks + KV_CHUNK] = s
            cm = jnp.maximum(s[:, :half], s[:, half:])
            mx = cm if mx is None else jnp.maximum(mx, cm)
        return jnp.max(mx, axis=1, keepdims=True)

    def probs(w, q0, rows, e, m):
        s_buf = s_scr.at[w % 2]
        p_buf = p_scr.at[w % 2]
        sm = None
        for ks in chunks(q0, rows):
            p = jnp.exp2(s_buf[0:rows, ks:ks + KV_CHUNK] - m)
            p_buf[0:rows, ks:ks + KV_CHUNK] = p.astype(BF16)
            ps = p[:, :half] + p[:, half:]
            sm = ps if sm is None else sm + ps
        return jnp.sum(sm, axis=1, keepdims=True)

    def values(w, q0, rows, e, l):
        n_keys = chunks(q0, rows).stop
        return _dot(p_scr[w % 2, 0:rows, 0:n_keys], v_ref[0:n_keys, :]) / l

    m_of, l_of, held = {}, {}, None
    for step in range(len(work) + 2):
        if step < len(work):
            m_of[step] = scores(step, *work[step])
        if 0 <= step - 1 < len(work):
            l_of[step - 1] = probs(step - 1, *work[step - 1], m_of.pop(step - 1))
        if 0 <= step - 2 < len(work):
            q0, rows, e = work[step - 2]
            o = values(step - 2, q0, rows, e, l_of.pop(step - 2))
            if e == 0:
                held = o
            else:
                lane = lax.broadcasted_iota(jnp.int32, o.shape, 1)
                o_ref[q0:q0 + rows, :] = jnp.where(lane < V_HEAD, held, o).astype(BF16)


def _attn_prompt(q, k, v):
    b, t, _ = q.shape
    t_pad = k.shape[1]
    pair = 2 * HEAD_PAD
    tile_rows = max(rows for _, rows in _query_tiles(t))
    return pl.pallas_call(
        functools.partial(_attn_kernel, t=t),
        grid=(b, N_HEADS // 2),
        in_specs=[pl.BlockSpec((None, t, pair), lambda bi, hi: (bi, 0, hi)),
                  pl.BlockSpec((None, t_pad, pair), lambda bi, hi: (bi, 0, hi)),
                  pl.BlockSpec((None, t_pad, 2 * V_HEAD), lambda bi, hi: (bi, 0, hi))],
        out_specs=pl.BlockSpec((None, t, 2 * V_HEAD), lambda bi, hi: (bi, 0, hi)),
        out_shape=jax.ShapeDtypeStruct((b, t, N_HEADS * V_HEAD), BF16),
        scratch_shapes=[pltpu.VMEM((2, tile_rows, t_pad), F32), pltpu.VMEM((2, tile_rows, t_pad), BF16)],
        compiler_params=_params("parallel", "parallel"),
        name="attn_prompt",
    )(q, k, v)


def _qlat_kernel(q_ref, w_ref, o_ref, r_ref):
    q = q_ref[...]
    o_ref[...] = _dot(q, w_ref[...]).astype(BF16)
    rot = q[:, QK_NOPE:].astype(F32)
    r_ref[...] = (rot[:, :QK_ROPE] + rot[:, QK_ROPE:]).astype(BF16)


def _q_latent(q, w_uk_t):
    m = q.shape[0]
    return pl.pallas_call(
        _qlat_kernel,
        grid=(N_HEADS,),
        in_specs=[pl.BlockSpec((m, HEAD_PAD), lambda h: (0, h)),
                  pl.BlockSpec((None, HEAD_PAD, KV_LORA), lambda h: (h, 0, 0))],
        out_specs=[pl.BlockSpec((None, m, KV_LORA), lambda h: (h, 0, 0)),
                   pl.BlockSpec((None, m, QK_ROPE), lambda h: (h, 0, 0))],
        out_shape=[jax.ShapeDtypeStruct((N_HEADS, m, KV_LORA), BF16),
                   jax.ShapeDtypeStruct((N_HEADS, m, QK_ROPE), BF16)],
        compiler_params=_params("parallel"),
        name="q_latent",
    )(q, w_uk_t)


def _decode_kernel(pt_ref, ql_ref, qr_ref, cn_ref, krn_ref, lat_hbm, krt_hbm, o_ref,
                   lat_buf, kr_buf, lat_sem, kr_sem, m_scr, l_scr, acc_scr, *, groups):
    n_seq = ql_ref.shape[0]
    n_iter = n_seq * groups // 2
    lanes = lambda a, n: jnp.concatenate([a] * n, axis=1)

    def copies(t, slot):
        b = t // groups
        first_page = (t % groups) * DEC_PAGES
        out = []
        for j in range(DEC_PAGES):
            page = pt_ref[b, first_page + j]
            out.append(pltpu.make_async_copy(
                lat_hbm.at[page], lat_buf.at[slot, pl.ds(j * PAGE_SIZE, PAGE_SIZE)], lat_sem.at[slot]))
            out.append(pltpu.make_async_copy(krt_hbm.at[page], kr_buf.at[slot, j], kr_sem.at[slot]))
        return out

    def start(t, slot):
        for cp in copies(t, slot):
            cp.start()

    def wait(t, slot):
        for cp in copies(t, slot):
            cp.wait()

    def group(slot, ql, qr):
        c = lat_buf[slot].astype(BF16)
        kr_t = jnp.concatenate([kr_buf[slot, j] for j in range(DEC_PAGES)], axis=1).astype(BF16)
        s = _dot_t(ql, c) + _dot(qr, kr_t)
        m = jnp.max(s, axis=1, keepdims=True)
        p = jnp.exp2(s - m)
        return m, jnp.sum(p, axis=1, keepdims=True), _dot(p.astype(BF16), c)

    start(0, 0)
    start(1, 1)

    def body(i, carry):
        base = (i % 2) * 2
        b = (2 * i) // groups
        ql = ql_ref[b]
        qr = qr_ref[b]

        @pl.when(i + 1 < n_iter)
        def _():
            start(2 * i + 2, 2 - base)
            start(2 * i + 3, 3 - base)

        @pl.when((2 * i) % groups == 0)
        def _():
            cn = cn_ref[b].astype(BF16).astype(F32)
            krn = krn_ref[b].astype(BF16).astype(F32)
            s_new = (jnp.sum(ql.astype(F32) * cn, axis=1, keepdims=True)
                     + jnp.sum(qr.astype(F32) * krn, axis=1, keepdims=True))
            m_scr[...] = jnp.broadcast_to(s_new, m_scr.shape)
            l_scr[...] = jnp.ones_like(l_scr)
            acc_scr[...] = jnp.broadcast_to(cn, acc_scr.shape)

        wait(2 * i, base)
        wait(2 * i + 1, base + 1)
        m0, l0, o0 = group(base, ql, qr)
        m1, l1, o1 = group(base + 1, ql, qr)

        m_prev = m_scr[...]
        m_new = jnp.maximum(m_prev, jnp.maximum(m0, m1))
        a_prev = jnp.exp2(m_prev - m_new)
        a0 = jnp.exp2(m0 - m_new)
        a1 = jnp.exp2(m1 - m_new)
        l_scr[...] = a_prev * l_scr[...] + a0 * l0 + a1 * l1
        acc_scr[...] = lanes(a_prev, 2) * acc_scr[...] + lanes(a0, 2) * o0 + lanes(a1, 2) * o1
        m_scr[...] = m_new

        @pl.when((2 * i + 2) % groups == 0)
        def _():
            o_ref[b] = acc_scr[...] / lanes(l_scr[...], 2)

        return carry

    lax.fori_loop(0, n_iter, body, 0)


def _decode_attn(page_table, q_lat, q_rope, c_new, kr_new, cache_latent, cache_krope_t):
    b, n_pages = page_table.shape
    groups = n_pages // DEC_PAGES
    assert n_pages % DEC_PAGES == 0 and groups % 2 == 0
    vmem = pl.BlockSpec(memory_space=pltpu.VMEM)
    return pl.pallas_call(
        functools.partial(_decode_kernel, groups=groups),
        in_specs=[pl.BlockSpec(memory_space=pltpu.SMEM), vmem, vmem, vmem, vmem,
                  pl.BlockSpec(memory_space=pl.ANY), pl.BlockSpec(memory_space=pl.ANY)],
        out_specs=vmem,
        out_shape=jax.ShapeDtypeStruct((b, N_HEADS, KV_LORA), F32),
        scratch_shapes=[
            pltpu.VMEM((DEC_SLOTS, DEC_PAGES * PAGE_SIZE, KV_LORA), F32),
            pltpu.VMEM((DEC_SLOTS, DEC_PAGES, QK_ROPE, PAGE_SIZE), F32),
            pltpu.SemaphoreType.DMA((DEC_SLOTS,)),
            pltpu.SemaphoreType.DMA((DEC_SLOTS,)),
            pltpu.VMEM((N_HEADS, HEAD_PAD), F32),
            pltpu.VMEM((N_HEADS, HEAD_PAD), F32),
            pltpu.VMEM((N_HEADS, KV_LORA), F32),
        ],
        compiler_params=pltpu.CompilerParams(vmem_limit_bytes=VMEM_LIMIT),
        name="decode_attn",
    )(page_table, q_lat, q_rope, c_new, kr_new, cache_latent, cache_krope_t)


def _olat_kernel(o_ref, w_ref, out_ref):
    out_ref[...] = (_dot(o_ref[0].astype(BF16), w_ref[0]) + _dot(o_ref[1].astype(BF16), w_ref[1])).astype(BF16)


def _o_latent_proj(o_lat, w_uv_pairs):
    m = o_lat.shape[1]
    return pl.pallas_call(
        _olat_kernel,
        grid=(N_HEADS // 2,),
        in_specs=[pl.BlockSpec((2, m, KV_LORA), lambda h: (h, 0, 0)),
                  pl.BlockSpec((2, KV_LORA, LANES), lambda h: (h, 0, 0))],
        out_specs=pl.BlockSpec((m, LANES), lambda h: (0, h)),
        out_shape=jax.ShapeDtypeStruct((m, N_HEADS * V_HEAD), BF16),
        compiler_params=_params("parallel"),
        name="o_latent_proj",
    )(o_lat, w_uv_pairs)


def _rope_tables(pos):
    half = QK_ROPE // 2
    inv = jnp.power(ROPE_BASE, -jnp.arange(half, dtype=F32) / half)
    ang = pos.astype(F32)[:, None] * inv[None, :]
    cos = jnp.cos(ang)
    sin = jnp.sin(ang)
    cos32 = jnp.concatenate([cos, cos], axis=1)
    sin32 = jnp.concatenate([sin, sin], axis=1)
    q_tab = Q_SCALE * jnp.concatenate([jnp.ones((pos.shape[0], QK_NOPE), F32), cos32, sin32], axis=1)
    return cos32, sin32, q_tab


def _rot_half_cols(w):
    half = QK_ROPE // 2
    return jnp.concatenate([-w[..., half:], w[..., :half]], axis=-1)


def _pad_heads(w, width):
    r = w.shape[0]
    return jnp.pad(w, ((0, 0), (0, 0), (0, HEAD_PAD - width))).reshape(r, HP)


def kernel(x_prompt, x_sample, cache_latent, cache_krope, state_conv, page_table, meta_tokens, a_norm_g, a_pw1_w, a_pw1_b, a_dw_w, a_dw_b, a_ln_g, a_ln_b, a_pw2_w, a_pw2_b, ffn_norm_g, ffn_w_gate, ffn_w_up, ffn_w_down, kv_norm_g, w_dkv, kv_latent_norm_g, w_uk, w_uv, b_norm_g, b_w_dq, b_q_norm_g, b_w_uq, b_w_o, final_norm_g):
    n_a = a_norm_g.shape[0]
    depth = ffn_norm_g.shape[0]
    n_b = depth - n_a
    bp, seq, _ = x_prompt.shape
    t_p = seq + N_META
    bs = x_sample.shape[0]
    past_len = page_table.shape[1] * PAGE_SIZE

    row = lambda v: v.reshape(1, -1).astype(F32)

    pw1_w = a_pw1_w.astype(BF16)
    pw2_w = a_pw2_w.astype(BF16)
    w_gate = ffn_w_gate.astype(BF16)
    w_up = ffn_w_up.astype(BF16)
    w_down = ffn_w_down.astype(BF16)
    first_tap = HIST - (CONV_W - 1)
    dw_tiles = jnp.pad(a_dw_w, ((0, 0), (first_tap, HIST + 8 - first_tap - CONV_W), (0, 0)))
    dw_tiles = dw_tiles.reshape(n_a, HIST + 8, 8, LANES)
    w_ckr = jnp.concatenate([w_dkv, _rot_half_cols(w_dkv[:, KV_LORA:])], axis=1).astype(BF16)
    w_k_pad = _pad_heads(w_uk, QK_NOPE).astype(BF16)
    w_v = w_uv.reshape(KV_LORA, N_HEADS * V_HEAD).astype(BF16)
    eye = jnp.eye(QK_ROPE, dtype=BF16)
    place = jnp.concatenate([jnp.zeros((QK_ROPE, QK_NOPE), BF16), eye, eye], axis=1)
    w_dq = b_w_dq.astype(BF16)
    w_q_pad = jnp.stack([
        _pad_heads(jnp.concatenate([b_w_uq[j], _rot_half_cols(b_w_uq[j][..., QK_NOPE:])], axis=-1), HEAD_PAD)
        for j in range(n_b)]).astype(BF16)
    w_o = b_w_o.astype(BF16)
    w_uk_t = jnp.pad(jnp.transpose(w_uk, (1, 2, 0)), ((0, 0), (0, HEAD_PAD - QK_NOPE), (0, 0))).astype(BF16)
    w_uv_h = jnp.transpose(w_uv, (1, 0, 2))
    w_uv_pairs = jnp.where((jnp.arange(N_HEADS) % 2 == 0)[:, None, None],
                           jnp.pad(w_uv_h, ((0, 0), (0, 0), (0, V_HEAD))),
                           jnp.pad(w_uv_h, ((0, 0), (0, 0), (V_HEAD, 0)))).astype(BF16)

    def ffn(x, l, tm, attn=None, window=None):
        final = row(final_norm_g) if l == depth - 1 else None
        return _ffn(x, row(ffn_norm_g[l]), w_gate, w_up, w_down, l, tm, attn=attn, final_g=final, window=window)

    def kv_args():
        return (row(kv_norm_g), w_ckr, row(kv_latent_norm_g))

    def q_args(j):
        return (row(b_norm_g[j]), w_dq, row(b_q_norm_g[j]), w_q_pad, j)

    xp = jnp.concatenate([jnp.broadcast_to(meta_tokens[None].astype(F32), (bp, N_META, D_MODEL)), x_prompt], axis=1)
    cos32, sin32, q_tab = _rope_tables(jnp.arange(t_p))
    m_p = bp * t_p
    t_pad = pl.cdiv(t_p, KV_CHUNK) * KV_CHUNK
    x = xp.reshape(m_p, D_MODEL)
    conv_p = []
    for l in range(n_a):
        glu = _pw1_glu(x, row(a_norm_g[l]), pw1_w, row(a_pw1_b[l]), PW1_TILE, tiles=True, layer=l)
        glu = glu.reshape(bp, t_p, 8, LANES)
        conv_p.append(glu[:, t_p - (CONV_W - 1):].reshape(bp, CONV_W - 1, D_MODEL))
        x = _conv_ffn(glu, x.reshape(bp, t_p, D_MODEL), dw_tiles[l], a_dw_b[l].reshape(8, LANES),
                      row(a_ln_g[l]), row(a_ln_b[l]), pw2_w, row(a_pw2_b[l]),
                      row(ffn_norm_g[l]), w_gate, w_up, w_down, FUSED_TILE, conv_layer=l, ffn_layer=l).reshape(m_p, D_MODEL)
    c_p, kr_p, k_p, v_p = _kv_side(x.reshape(bp, t_p, D_MODEL), *kv_args(), cos32, sin32, SEQ_TILE,
                                   heads=(w_k_pad, w_v, place), t_pad=t_pad)
    for j in range(n_b):
        q = _q_proj(x.reshape(bp, t_p, D_MODEL), *q_args(j), q_tab, SEQ_TILE)
        o = _attn_prompt(q, k_p, v_p).reshape(m_p, N_HEADS * V_HEAD)
        if n_a + j < depth - 1:
            x = ffn(x, n_a + j, ROW_TILE, attn=(o, w_o, j))
        else:
            x = ffn(x, n_a + j, OUT_TILE, attn=(o, w_o, j), window=(bp, t_p, N_META))
    y_prompt = x.reshape(bp, seq, D_MODEL)

    pos_s = jnp.full((bs,), past_len, jnp.int32)
    cos32s, sin32s, q_tab_s = _rope_tables(pos_s)
    x = x_sample.reshape(bs, D_MODEL)
    conv_s = []
    state_t = jnp.transpose(state_conv, (0, 2, 1, 3))
    krope_t = jnp.swapaxes(cache_krope, 1, 2)
    for l in range(n_a):
        glu = _pw1_glu(x, row(a_norm_g[l]), pw1_w, row(a_pw1_b[l]), bs, tiles=False, layer=l)
        conv_s.append(jnp.concatenate([state_conv[l][:, 1:], glu[:, None, :]], axis=1))
        x = _conv_step(state_t[l], glu, x, a_dw_w[l], row(a_dw_b[l]),
                       row(a_ln_g[l]), row(a_ln_b[l]), pw2_w[l], row(a_pw2_b[l]))
        x = ffn(x, l, bs)
    c_s, kr_s = _kv_side(x[None], *kv_args(), cos32s, sin32s, bs)
    c_s = c_s.reshape(bs, 1, KV_LORA)
    kr_s = kr_s.reshape(bs, 1, QK_ROPE)
    for j in range(n_b):
        q = _q_proj(x[None], *q_args(j), q_tab_s, bs)[0]
        q_lat, q_rope = _q_latent(q, w_uk_t)
        o_lat = _decode_attn(page_table, jnp.transpose(q_lat, (1, 0, 2)), jnp.transpose(q_rope, (1, 0, 2)),
                             c_s, kr_s, cache_latent, krope_t)
        o = _o_latent_proj(jnp.transpose(o_lat, (1, 0, 2)), w_uv_pairs)
        x = ffn(x, n_a + j, bs, attn=(o, w_o, j))
    y_sample = x.reshape(bs, 1, D_MODEL)

    return (y_prompt, y_sample, c_p, kr_p, jnp.stack(conv_p), c_s, kr_s, jnp.stack(conv_s))
```
